```python
import math
import jax, jax.numpy as jnp
from jax import lax
import numpy as np

D_MODEL = 1024
BATCH = 32
SEQ = 2048
DEPTH = 1

N_META = 16
EPS = 1e-6
BLOCK = 128
D_RNN = D_MODEL
RG_BLOCKS = 8
RG_BS = D_RNN // RG_BLOCKS
CONV_W = 4
RG_C = 8.0
MLA_HEADS = 16
Q_LORA = 384
KV_LORA = 256
QK_NOPE = 64
QK_ROPE = 32
V_DIM = 64
ROPE_BASE = 10000.0
PEER_HEADS = 8
PEER_NKEYS = 128
PEER_EXPERTS = PEER_NKEYS * PEER_NKEYS
PEER_QDIM = 256
PEER_TOPK = 16

IN_SIZES = (D_RNN, D_RNN, Q_LORA, KV_LORA, QK_ROPE, D_MODEL, D_MODEL)
IN_SPLITS = tuple(int(s) for s in np.cumsum(IN_SIZES)[:-1])
D_IN = int(sum(IN_SIZES))

kernel_name = 'hybrid_rglru_mla_peer_block'


def _rmsnorm(x, g):
    xf = x.astype(jnp.float32)
    xf = xf * lax.rsqrt(jnp.mean(xf * xf, axis=-1, keepdims=True) + EPS)
    return xf.astype(x.dtype) * g


def _rope(x, pos):
    half = QK_ROPE // 2
    inv = jnp.power(ROPE_BASE, -jnp.arange(half, dtype=jnp.float32) * (2.0 / QK_ROPE))
    ang = pos[:, None] * inv[None, :]
    cos = jnp.cos(ang)[None, :, None, :].astype(x.dtype)
    sin = jnp.sin(ang)[None, :, None, :].astype(x.dtype)
    x1, x2 = x[..., :half], x[..., half:]
    return jnp.concatenate([x1 * cos - x2 * sin, x1 * sin + x2 * cos], axis=-1)


def _causal_conv(x, w, b):
    T = x.shape[1]
    xp = jnp.pad(x, ((0, 0), (CONV_W - 1, 0), (0, 0)))
    y = xp[:, 0:T] * w[0] + b
    for k in range(1, CONV_W):
        y = y + xp[:, k:k + T] * w[k]
    return y


def _lin_combine(e1, e2):
    a1, b1 = e1
    a2, b2 = e2
    return (a1 * a2, a2 * b1 + b2)


def _rg_lru(x, w_a, b_a, w_x, b_x, lam):
    B, T, _ = x.shape
    xb = x.reshape(B, T, RG_BLOCKS, RG_BS)
    r = jax.nn.sigmoid(jnp.einsum('btnc,ncd->btnd', xb, w_a).reshape(B, T, D_RNN) + b_a)
    i = jax.nn.sigmoid(jnp.einsum('btnc,ncd->btnd', xb, w_x).reshape(B, T, D_RNN) + b_x)
    log_a = RG_C * r.astype(jnp.float32) * jax.nn.log_sigmoid(lam.astype(jnp.float32))
    a = jnp.exp(log_a)
    u = jnp.sqrt(-jnp.expm1(2.0 * log_a)) * (i * x).astype(jnp.float32)
    _, h = lax.associative_scan(_lin_combine, (a, u), axis=1)
    return h.astype(x.dtype)


def _mla_attention(q_nope, q_rope, k_nope, k_rope, v):
    B, T, H, _ = q_nope.shape
    nb = T // BLOCK
    scale = (QK_NOPE + QK_ROPE) ** -0.5
    key_pos = jnp.arange(T)

    def to_blocks(t):
        return jnp.swapaxes(t.reshape((B, nb, BLOCK) + t.shape[2:]), 0, 1)

    def one_block(args):
        qn, qr, bi = args
        q_pos = bi * BLOCK + jnp.arange(BLOCK)
        s = (jnp.einsum('bqhd,bkhd->bhqk', qn, k_nope)
             + jnp.einsum('bqhd,bkd->bhqk', qr, k_rope)).astype(jnp.float32) * scale
        s = jnp.where(key_pos[None, :] <= q_pos[:, None], s, -1e30)
        p = jax.nn.softmax(s, axis=-1).astype(v.dtype)
        return jnp.einsum('bhqk,bkhd->bqhd', p, v)

    o = lax.map(one_block, (to_blocks(q_nope), to_blocks(q_rope), jnp.arange(nb)))
    return jnp.swapaxes(o, 0, 1).reshape(B, T, H * V_DIM)


def _peer(x, w_q, keys1, keys2, u, v):
    B, T, D = x.shape
    half = PEER_QDIM // 2
    xc = x.reshape(B * T // BLOCK, BLOCK, D)

    def one_chunk(xt):
        q = (xt @ w_q).reshape(BLOCK, PEER_HEADS, PEER_QDIM)
        s1 = jnp.einsum('chd,nd->chn', q[..., :half], keys1).astype(jnp.float32)
        s2 = jnp.einsum('chd,nd->chn', q[..., half:], keys2).astype(jnp.float32)
        t1, i1 = lax.top_k(s1, PEER_TOPK)
        t2, i2 = lax.top_k(s2, PEER_TOPK)
        cand = (t1[..., :, None] + t2[..., None, :]).reshape(BLOCK, PEER_HEADS, PEER_TOPK * PEER_TOPK)
        cidx = (i1[..., :, None] * PEER_NKEYS + i2[..., None, :]).reshape(BLOCK, PEER_HEADS, PEER_TOPK * PEER_TOPK)
        ts, sel = lax.top_k(cand, PEER_TOPK)
        idx = jnp.take_along_axis(cidx, sel, axis=-1)
        g = jax.nn.softmax(ts, axis=-1)
        act = jax.nn.gelu(jnp.einsum('chkd,cd->chk', u[idx], xt))
        coef = (g * act.astype(jnp.float32)).astype(xt.dtype)
        return jnp.einsum('chk,chkd->cd', coef, v[idx])

    return lax.map(one_chunk, xc).reshape(B, T, D)


def setup_inputs(seed: int = 0) -> dict:
    key = jax.random.key(seed)
    ks = jax.random.split(key, 26)
    f32 = jnp.float32
    L = DEPTH

    def nrm(k, shape, scale):
        return jax.random.normal(k, shape, f32) * scale

    def gain(k, shape):
        return 1.0 + 0.01 * jax.random.normal(k, shape, f32)

    a_c = jax.random.uniform(ks[10], (L, D_RNN), f32, 0.9, 0.999)
    s = a_c ** (1.0 / RG_C)
    rg_lambda = jnp.log(s) - jnp.log1p(-s)
    return {
        'x': nrm(ks[0], (BATCH, SEQ, D_MODEL), 1.0),
        'meta_tokens': nrm(ks[1], (N_META, D_MODEL), 1.0),
        'norm1_g': gain(ks[2], (L, D_MODEL)),
        'w_in': nrm(ks[3], (L, D_MODEL, D_IN), D_MODEL ** -0.5),
        'conv_w': nrm(ks[4], (L, CONV_W, D_RNN), CONV_W ** -0.5),
        'conv_b': nrm(ks[5], (L, D_RNN), 0.01),
        'rg_wa': nrm(ks[6], (L, RG_BLOCKS, RG_BS, RG_BS), RG_BS ** -0.5),
        'rg_ba': nrm(ks[7], (L, D_RNN), 0.01),
        'rg_wx': nrm(ks[8], (L, RG_BLOCKS, RG_BS, RG_BS), RG_BS ** -0.5),
        'rg_bx': nrm(ks[9], (L, D_RNN), 0.01),
        'rg_lambda': rg_lambda,
        'w_rnn_out': nrm(ks[11], (L, D_RNN, D_MODEL), D_RNN ** -0.5),
        'q_norm_g': gain(ks[12], (L, Q_LORA)),
        'w_uq': nrm(ks[13], (L, Q_LORA, MLA_HEADS * (QK_NOPE + QK_ROPE)), Q_LORA ** -0.5),
        'kv_norm_g': gain(ks[14], (L, KV_LORA)),
        'w_ukv': nrm(ks[15], (L, KV_LORA, MLA_HEADS * (QK_NOPE + V_DIM)), KV_LORA ** -0.5),
        'w_attn_out': nrm(ks[16], (L, MLA_HEADS * V_DIM, D_MODEL), (MLA_HEADS * V_DIM) ** -0.5),
        'w_out': nrm(ks[17], (L, D_MODEL, D_MODEL), D_MODEL ** -0.5),
        'norm2_g': gain(ks[18], (L, D_MODEL)),
        'peer_wq': nrm(ks[19], (L, D_MODEL, PEER_HEADS * PEER_QDIM), D_MODEL ** -0.5),
        'peer_keys1': nrm(ks[20], (L, PEER_NKEYS, PEER_QDIM // 2), (PEER_QDIM // 2) ** -0.5),
        'peer_keys2': nrm(ks[21], (L, PEER_NKEYS, PEER_QDIM // 2), (PEER_QDIM // 2) ** -0.5),
        'peer_u': nrm(ks[22], (L, PEER_EXPERTS, D_MODEL), D_MODEL ** -0.5),
        'peer_v': nrm(ks[23], (L, PEER_EXPERTS, D_MODEL), PEER_HEADS ** -0.5),
        'final_g': gain(ks[24], (D_MODEL,)),
    }


def reference(x, meta_tokens, norm1_g, w_in, conv_w, conv_b, rg_wa, rg_ba, rg_wx, rg_bx, rg_lambda,
              w_rnn_out, q_norm_g, w_uq, kv_norm_g, w_ukv, w_attn_out, w_out, norm2_g,
              peer_wq, peer_keys1, peer_keys2, peer_u, peer_v, final_g):
    B, S, D = x.shape
    T = N_META + S
    Tp = -(-T // BLOCK) * BLOCK
    meta = jnp.broadcast_to(meta_tokens[None].astype(x.dtype), (B, N_META, D))
    h = jnp.concatenate([meta, x], axis=1)
    h = jnp.pad(h, ((0, 0), (0, Tp - T), (0, 0)))
    pos = jnp.arange(Tp, dtype=jnp.float32)
    for l in range(DEPTH):
        n1 = _rmsnorm(h, norm1_g[l])
        xr, gr, cq, ckv, kr, g_rnn, g_att = jnp.split(n1 @ w_in[l], IN_SPLITS, axis=-1)
        hr = _rg_lru(_causal_conv(xr, conv_w[l], conv_b[l]), rg_wa[l], rg_ba[l], rg_wx[l], rg_bx[l], rg_lambda[l])
        y_rnn = (hr * jax.nn.gelu(gr)) @ w_rnn_out[l]
        q = (_rmsnorm(cq, q_norm_g[l]) @ w_uq[l]).reshape(B, Tp, MLA_HEADS, QK_NOPE + QK_ROPE)
        kv = (_rmsnorm(ckv, kv_norm_g[l]) @ w_ukv[l]).reshape(B, Tp, MLA_HEADS, QK_NOPE + V_DIM)
        q_rope = _rope(q[..., QK_NOPE:], pos)
        k_rope = _rope(kr[:, :, None, :], pos)[:, :, 0, :]
        o = _mla_attention(q[..., :QK_NOPE], q_rope, kv[..., :QK_NOPE], k_rope, kv[..., QK_NOPE:])
        y_att = o @ w_attn_out[l]
        mixed = jax.nn.sigmoid(g_rnn) * y_rnn + jax.nn.sigmoid(g_att) * y_att
        h = h + mixed @ w_out[l]
        h = h + _peer(_rmsnorm(h, norm2_g[l]), peer_wq[l], peer_keys1[l], peer_keys2[l], peer_u[l], peer_v[l])
    h = _rmsnorm(h, final_g)
    return h[:, N_META:N_META + S]
```

```python
import functools

import jax
import jax.numpy as jnp
import numpy as np
from jax import lax
from jax.experimental import pallas as pl
from jax.experimental.pallas import tpu as pltpu

N_META = 16
EPS = 1e-6
BLOCK = 128
RG_BLOCKS = 8
CONV_W = 4
RG_C = 8.0
MLA_HEADS = 16
Q_LORA = 384
KV_LORA = 256
QK_NOPE = 64
QK_ROPE = 32
V_DIM = 64
ROPE_BASE = 10000.0
PEER_HEADS = 8
PEER_NKEYS = 128
PEER_TOPK = 16

LANES = 128
SUBLANES = 8
HEAD_PAD = 128
VMEM_LIMIT = 56 * 1024 * 1024
NKH = PEER_NKEYS * PEER_HEADS

_CAND = [(k1, k2) for k1 in range(PEER_TOPK) for k2 in range(PEER_TOPK)
         if (k1 + 1) * (k2 + 1) <= PEER_TOPK]
_NCAND = len(_CAND)


def _rms(x, g):
    return x * lax.rsqrt(jnp.mean(x * x, axis=-1, keepdims=True) + EPS) * g


def _bf(x):
    return x.astype(jnp.bfloat16)


def _dot(a, b):
    return jnp.dot(a, b, preferred_element_type=jnp.float32)


def _dot_nt(a, b):
    return lax.dot_general(a, b, (((1,), (1,)), ((), ())), preferred_element_type=jnp.float32)


def _const_spec(shape):
    nd = len(shape)
    return pl.BlockSpec(shape, lambda *_: (0,) * nd, pipeline_mode=pl.Buffered(1))


def _row(a):
    return a.reshape(1, -1)


def _proj_kernel(h_ref, g1_ref, w1_ref, w2_ref, qg_ref, kvg_ref, wuq_ref, wuk_ref, wuv_ref,
                 ct_ref, st_ref,
                 xr_ref, gr_ref, grnn_ref, gatt_ref, q_ref, k_ref, v_ref):
    D = h_ref.shape[1]
    n1 = _bf(_rms(h_ref[...], g1_ref[...]))
    for idx, o_ref in enumerate((xr_ref, gr_ref, grnn_ref, gatt_ref)):
        o_ref[...] = _bf(_dot(n1, w1_ref[:, idx * D:(idx + 1) * D]))
    lat = _dot(n1, w2_ref[...])
    cq = lat[:, :Q_LORA]
    ckv = lat[:, Q_LORA:Q_LORA + KV_LORA]
    kr = lat[:, Q_LORA + KV_LORA:]
    ct = ct_ref[...]
    st = st_ref[...]

    def rope(blk):
        return blk * ct + pltpu.roll(blk, HEAD_PAD - QK_ROPE, 1) * st

    nq = _bf(_rms(cq, qg_ref[...]))
    nkv = _bf(_rms(ckv, kvg_ref[...]))
    kr_rot = rope(kr)
    for h in range(MLA_HEADS):
        sl = slice(h * HEAD_PAD, (h + 1) * HEAD_PAD)
        q_ref[:, sl] = _bf(rope(_dot(nq, wuq_ref[:, sl])))
        k_ref[:, sl] = _bf(_dot(nkv, wuk_ref[:, sl]) + kr_rot)
    v_ref[...] = _bf(_dot(nkv, wuv_ref[...]))


def _half_swap(w):
    half = QK_ROPE // 2
    return jnp.concatenate([-w[..., half:], w[..., :half]], axis=-1)


def _stage_proj(h0, Tp, norm1_g, w_in, q_norm_g, w_uq, kv_norm_g, w_ukv):
    N, D = h0.shape
    f32, bf16 = jnp.float32, jnp.bfloat16
    H = MLA_HEADS
    sizes = (D, D, Q_LORA, KV_LORA, QK_ROPE, D, D)
    offs = np.cumsum((0,) + sizes)
    w_xr, w_gr, w_cq, w_ckv, w_kr, w_grnn, w_gatt = [w_in[:, offs[i]:offs[i + 1]] for i in range(7)]
    w1 = _bf(jnp.concatenate([w_xr, w_gr, w_grnn, w_gatt], axis=1))
    w_kr_ext = jnp.concatenate([jnp.zeros((D, QK_NOPE), f32), w_kr, _half_swap(w_kr)], axis=1)
    w2 = _bf(jnp.concatenate([w_cq, w_ckv, w_kr_ext], axis=1))
    wuq = w_uq.reshape(Q_LORA, H, QK_NOPE + QK_ROPE)
    wuq_ext = _bf(jnp.concatenate([wuq, _half_swap(wuq[..., QK_NOPE:])], axis=-1).reshape(Q_LORA, H * HEAD_PAD))
    wukv = w_ukv.reshape(KV_LORA, H, QK_NOPE + V_DIM)
    wuk_ext = _bf(jnp.concatenate([wukv[..., :QK_NOPE], jnp.zeros((KV_LORA, H, HEAD_PAD - QK_NOPE), f32)],
                                  axis=-1).reshape(KV_LORA, H * HEAD_PAD))
    wuv = _bf(wukv[..., QK_NOPE:].reshape(KV_LORA, H * V_DIM))

    half = QK_ROPE // 2
    pos = jnp.arange(Tp, dtype=f32)
    inv = jnp.power(ROPE_BASE, -jnp.arange(half, dtype=f32) * (2.0 / QK_ROPE))
    ang = pos[:, None] * inv[None, :]
    cos, sin = jnp.cos(ang), jnp.sin(ang)
    ctab = jnp.concatenate([jnp.ones((Tp, QK_NOPE), f32), cos, cos, jnp.zeros((Tp, QK_ROPE), f32)], axis=1)
    stab = jnp.concatenate([jnp.zeros((Tp, QK_NOPE), f32), sin, sin, jnp.zeros((Tp, QK_ROPE), f32)], axis=1)

    TM = Tp // 4
    npb = Tp // TM
    tok = lambda width: pl.BlockSpec((TM, width), lambda i: (i, 0))
    return pl.pallas_call(
        _proj_kernel,
        grid=(N // TM,),
        in_specs=[tok(D), _const_spec((1, D)), _const_spec(w1.shape), _const_spec(w2.shape),
                  _const_spec((1, Q_LORA)), _const_spec((1, KV_LORA)),
                  _const_spec(wuq_ext.shape), _const_spec(wuk_ext.shape), _const_spec(wuv.shape),
                  pl.BlockSpec((TM, HEAD_PAD), lambda i: (i % npb, 0)),
                  pl.BlockSpec((TM, HEAD_PAD), lambda i: (i % npb, 0))],
        out_specs=[tok(D), tok(D), tok(D), tok(D), tok(H * HEAD_PAD), tok(H * HEAD_PAD), tok(H * V_DIM)],
        out_shape=[jax.ShapeDtypeStruct((N, D), bf16)] * 4
                  + [jax.ShapeDtypeStruct((N, H * HEAD_PAD), bf16)] * 2
                  + [jax.ShapeDtypeStruct((N, H * V_DIM), bf16)],
        compiler_params=pltpu.CompilerParams(dimension_semantics=("parallel",), vmem_limit_bytes=VMEM_LIMIT),
        name="proj",
    )(h0, _row(norm1_g), w1, w2, _row(q_norm_g), _row(kv_norm_g), wuq_ext, wuk_ext, wuv, ctab, stab)


def _rglru_kernel(xr_ref, gr_ref, grnn_ref, cw_ref, cb_ref, wa_ref, ba_ref, wx_ref, bx_ref,
                  lam_ref, wo_ref, out_ref, xe_ref, hs_ref):
    Tc, D = xr_ref.shape
    c = pl.program_id(1)

    @pl.when(c == 0)
    def _():
        xe_ref[0:SUBLANES, :] = jnp.zeros((SUBLANES, D), jnp.float32)
        hs_ref[...] = jnp.zeros_like(hs_ref)

    x = xr_ref[...].astype(jnp.float32)
    xe_ref[SUBLANES:, :] = x
    y = cb_ref[...] + x * cw_ref[CONV_W - 1:CONV_W, :]
    for kk in range(CONV_W - 1):
        off = SUBLANES - (CONV_W - 1) + kk
        y = y + xe_ref[off:off + Tc, :] * cw_ref[kk:kk + 1, :]
    xe_ref[0:SUBLANES, :] = xe_ref[Tc:Tc + SUBLANES, :]

    yb = _bf(y)
    bs = D // RG_BLOCKS
    r = jnp.concatenate([_dot(yb[:, n * bs:(n + 1) * bs], wa_ref[n]) for n in range(RG_BLOCKS)], axis=1)
    i = jnp.concatenate([_dot(yb[:, n * bs:(n + 1) * bs], wx_ref[n]) for n in range(RG_BLOCKS)], axis=1)
    r = jax.nn.sigmoid(r + ba_ref[...])
    i = jax.nn.sigmoid(i + bx_ref[...])
    lam = lam_ref[...]
    log_sig = jnp.minimum(lam, 0.0) - jnp.log(1.0 + jnp.exp(-jnp.abs(lam)))
    log_a = RG_C * r * log_sig
    a = jnp.exp(log_a)
    u = jnp.sqrt(1.0 - jnp.exp(2.0 * log_a)) * (i * y)

    row = lax.broadcasted_iota(jnp.int32, (Tc, D), 0)
    d = 1
    while d < Tc:
        keep = row >= d
        a_prev = jnp.where(keep, pltpu.roll(a, d, 0), 1.0)
        u_prev = jnp.where(keep, pltpu.roll(u, d, 0), 0.0)
        u = a * u_prev + u
        a = a * a_prev
        d *= 2
    h = u + a * hs_ref[0:1, :]
    hs_ref[...] = jnp.broadcast_to(h[Tc - 1:Tc, :], hs_ref.shape)

    gate = jax.nn.gelu(gr_ref[...].astype(jnp.float32))
    y_rnn = _dot(_bf(h * gate), wo_ref[...])
    out_ref[...] = _bf(jax.nn.sigmoid(grnn_ref[...].astype(jnp.float32)) * y_rnn)


def _stage_rglru(xr, gr, grnn, B, Tp, conv_w, conv_b, rg_wa, rg_ba, rg_wx, rg_bx, rg_lambda, w_rnn_out):
    N, D = xr.shape
    Tc = Tp // 4
    ncb = Tp // Tc
    seq = lambda: pl.BlockSpec((Tc, D), lambda b, c: (b * ncb + c, 0))
    return pl.pallas_call(
        _rglru_kernel,
        grid=(B, ncb),
        in_specs=[seq(), seq(), seq(), _const_spec((CONV_W, D)), _const_spec((1, D)),
                  _const_spec(rg_wa.shape), _const_spec((1, D)),
                  _const_spec(rg_wx.shape), _const_spec((1, D)), _const_spec((1, D)),
                  _const_spec((D, D))],
        out_specs=seq(),
        out_shape=jax.ShapeDtypeStruct((N, D), jnp.bfloat16),
        scratch_shapes=[pltpu.VMEM((Tc + SUBLANES, D), jnp.float32), pltpu.VMEM((SUBLANES, D), jnp.float32)],
        compiler_params=pltpu.CompilerParams(dimension_semantics=("parallel", "arbitrary"),
                                             vmem_limit_bytes=VMEM_LIMIT),
        name="rglru",
    )(xr, gr, grnn, conv_w, _row(conv_b), _bf(rg_wa), _row(rg_ba), _bf(rg_wx), _row(rg_bx),
      _row(rg_lambda), _bf(w_rnn_out))


def _attn_kernel(q_ref, k_ref, v_ref, o_ref, *, q_blocks, scale):
    for j in range(2):
        for (qs, qn) in q_blocks:
            kl = qs + qn
            qh = q_ref[0, qs:qs + qn, j * HEAD_PAD:(j + 1) * HEAD_PAD]
            kh = k_ref[0, 0:kl, j * HEAD_PAD:(j + 1) * HEAD_PAD]
            vh = v_ref[0, 0:kl, j * V_DIM:(j + 1) * V_DIM]
            s = _dot_nt(qh, kh) * scale
            rowi = lax.broadcasted_iota(jnp.int32, (qn, kl), 0) + qs
            coli = lax.broadcasted_iota(jnp.int32, (qn, kl), 1)
            s = jnp.where(coli <= rowi, s, -1e30)
            m = jnp.max(s, axis=-1, keepdims=True)
            p = jnp.exp(s - m)
            l = jnp.sum(p, axis=-1, keepdims=True)
            o = _dot(_bf(p), vh) / l
            o_ref[0, qs:qs + qn, j * V_DIM:(j + 1) * V_DIM] = _bf(o)


def _q_blocks(tp, qb):
    blocks = []
    s = 0
    while s < tp:
        n = min(qb, tp - s)
        blocks.append((s, n))
        s += n
    return tuple(blocks)


def _stage_attn(q, k, v, B, Tp):
    H = MLA_HEADS
    q3 = q.reshape(B, Tp, H * HEAD_PAD)
    k3 = k.reshape(B, Tp, H * HEAD_PAD)
    v3 = v.reshape(B, Tp, H * V_DIM)
    o = pl.pallas_call(
        functools.partial(_attn_kernel, q_blocks=_q_blocks(Tp, 2 * BLOCK),
                          scale=float((QK_NOPE + QK_ROPE) ** -0.5)),
        grid=(B, H // 2),
        in_specs=[pl.BlockSpec((1, Tp, 2 * HEAD_PAD), lambda b, hp: (b, 0, hp)),
                  pl.BlockSpec((1, Tp, 2 * HEAD_PAD), lambda b, hp: (b, 0, hp)),
                  pl.BlockSpec((1, Tp, 2 * V_DIM), lambda b, hp: (b, 0, hp))],
        out_specs=pl.BlockSpec((1, Tp, 2 * V_DIM), lambda b, hp: (b, 0, hp)),
        out_shape=jax.ShapeDtypeStruct((B, Tp, H * V_DIM), jnp.bfloat16),
        compiler_params=pltpu.CompilerParams(dimension_semantics=("parallel", "parallel"),
                                             vmem_limit_bytes=VMEM_LIMIT),
        name="attn",
    )(q3, k3, v3)
    return o.reshape(B * Tp, H * V_DIM)


def _merge_kernel(h_ref, yr_ref, o_ref, gatt_ref, wao_ref, wout_ref, g2_ref, wq_ref, k1_ref, k2_ref,
                  h1_ref, n2t_ref, c_ref, q1_ref, r2_ref, e2_ref,
                  s_ref, r_ref, e_ref, t_ref, cd_ref, sel_ref, msk_ref):
    TM = h_ref.shape[0]
    LG = TM // LANES
    NK = PEER_NKEYS
    H = PEER_HEADS
    f32 = jnp.float32

    y_att = _dot(o_ref[...], wao_ref[...])
    mixed = yr_ref[...].astype(f32) + jax.nn.sigmoid(gatt_ref[...].astype(f32)) * y_att
    h1 = h_ref[...] + _dot(_bf(mixed), wout_ref[...])
    h1_ref[...] = h1
    n2 = _bf(_rms(h1, g2_ref[...]))
    n2t_ref[...] = _bf(jnp.transpose(n2.astype(f32)))
    qp = _bf(_dot(n2, wq_ref[...]))

    for h in range(H):
        for side, kref in enumerate((k1_ref, k2_ref)):
            qs = qp[:, (2 * h + side) * NK:(2 * h + side + 1) * NK]
            st = _dot_nt(kref[...], qs)
            for lg in range(LG):
                s_ref[lg, pl.ds(side * NKH + h, NK, stride=H), :] = st[:, lg * LANES:(lg + 1) * LANES]

    def side_rows(side):
        return slice(side * NKH, (side + 1) * NKH)

    def load4(ref, side):
        return ref[:, side_rows(side), :].reshape(LG, NK, H, LANES)

    def store4(ref, side, val):
        ref[:, side_rows(side), :] = val.reshape(LG, NKH, LANES)

    for side in range(2):
        s0 = load4(s_ref, side)
        store4(e_ref, side, jnp.exp(s0 - jnp.max(s0, axis=1, keepdims=True)))
        store4(r_ref, side, jnp.full((LG, NK, H, LANES), float(PEER_TOPK), f32))

    key_iota = lax.broadcasted_iota(jnp.int32, (LG, NK, H, LANES), 1)

    def topk_step(k, carry):
        for side in range(2):
            s = load4(s_ref, side)
            m = jnp.max(s, axis=1, keepdims=True)
            first = jnp.min(jnp.where(s == m, key_iota, NK), axis=1, keepdims=True)
            hit = key_iota == first
            store4(r_ref, side, jnp.where(hit, k.astype(f32), load4(r_ref, side)))
            store4(s_ref, side, jnp.where(hit, -jnp.inf, s))
            t_ref[k, side] = m[:, 0]
        return carry

    lax.fori_loop(0, PEER_TOPK, topk_step, 0)

    for p, (k1, k2) in enumerate(_CAND):
        cd_ref[p] = t_ref[k1, 0] + t_ref[k2, 1]
    cd = cd_ref[...]
    c0 = cd_ref[0]
    p_iota = lax.broadcasted_iota(jnp.int32, cd.shape, 0)

    def rank_step(p, carry):
        cp = cd_ref[p]
        beats = jnp.where(p_iota < p, jnp.where(cd >= cp, 1.0, 0.0), jnp.where(cd > cp, 1.0, 0.0))
        rank = jnp.sum(beats, axis=0)
        sel = jnp.where(rank < float(PEER_TOPK), 1.0, 0.0)
        msk_ref[p] = sel
        sel_ref[p] = sel * jnp.exp(cp - c0)
        return carry

    lax.fori_loop(0, _NCAND, rank_step, 0)

    z = jnp.sum(sel_ref[...], axis=0)
    seln = msk_ref[...]
    counts = []
    p = 0
    for k1 in range(PEER_TOPK):
        n = sum(1 for (a, _) in _CAND if a == k1)
        counts.append(jnp.sum(seln[p:p + n], axis=0))
        p += n

    r1 = load4(r_ref, 0)
    cc = jnp.zeros((LG, NK, H, LANES), f32)
    for k1 in range(PEER_TOPK):
        cc = jnp.where(r1 == float(k1), counts[k1][:, None], cc)
    q1 = load4(e_ref, 0) / z[:, None]
    for lg in range(LG):
        cols = slice(lg * LANES, (lg + 1) * LANES)
        c_ref[:, cols] = cc[lg].reshape(NKH, LANES)
        q1_ref[:, cols] = q1[lg].reshape(NKH, LANES)
        for h in range(H):
            r2_ref[h * NK:(h + 1) * NK, cols] = r_ref[lg, pl.ds(NKH + h, NK, stride=H), :]
            e2_ref[h * NK:(h + 1) * NK, cols] = e_ref[lg, pl.ds(NKH + h, NK, stride=H), :]


def _stage_merge(h0, yrg, o, gatt, w_attn_out, w_out, norm2_g, peer_wq, peer_keys1, peer_keys2):
    N, D = h0.shape
    f32 = jnp.float32
    TM = 256
    LG = TM // LANES
    tok = lambda width: pl.BlockSpec((TM, width), lambda i: (i, 0))
    colb = lambda rows: pl.BlockSpec((rows, TM), lambda i: (0, i))
    return pl.pallas_call(
        _merge_kernel,
        grid=(N // TM,),
        in_specs=[tok(D), tok(D), tok(D), tok(D), _const_spec((D, D)), _const_spec((D, D)),
                  _const_spec((1, D)), _const_spec(peer_wq.shape),
                  _const_spec(peer_keys1.shape), _const_spec(peer_keys2.shape)],
        out_specs=[tok(D), colb(D), colb(NKH), colb(NKH), colb(NKH), colb(NKH)],
        out_shape=[jax.ShapeDtypeStruct((N, D), f32), jax.ShapeDtypeStruct((D, N), jnp.bfloat16)]
                  + [jax.ShapeDtypeStruct((NKH, N), f32)] * 4,
        scratch_shapes=[pltpu.VMEM((LG, 2 * NKH, LANES), f32), pltpu.VMEM((LG, 2 * NKH, LANES), f32),
                        pltpu.VMEM((LG, 2 * NKH, LANES), f32),
                        pltpu.VMEM((PEER_TOPK, 2, LG, PEER_HEADS, LANES), f32),
                        pltpu.VMEM((_NCAND, LG, PEER_HEADS, LANES), f32),
                        pltpu.VMEM((_NCAND, LG, PEER_HEADS, LANES), f32),
                        pltpu.VMEM((_NCAND, LG, PEER_HEADS, LANES), f32)],
        compiler_params=pltpu.CompilerParams(dimension_semantics=("parallel",), vmem_limit_bytes=VMEM_LIMIT),
        name="merge",
    )(h0, yrg, o, gatt, _bf(w_attn_out), _bf(w_out), _row(norm2_g), _bf(peer_wq),
      _bf(peer_keys1), _bf(peer_keys2))


def _peer_kernel(n2t_ref, c_ref, q1_ref, r2_ref, e2_ref, u_ref, vt_ref, h1_ref, gf_ref,
                 out_ref, acc_ref):
    j = pl.program_id(1)
    ET = u_ref.shape[0]
    TT = n2t_ref.shape[1]
    NK = PEER_NKEYS
    H = PEER_HEADS
    groups = ET // NK

    @pl.when(j == 0)
    def _():
        acc_ref[...] = jnp.zeros_like(acc_ref)

    act = _dot(u_ref[...], n2t_ref[...])
    gl = jax.nn.gelu(act)
    coef = []
    for g in range(groups):
        row0 = pl.multiple_of((j * groups + g) * H, H)
        crow = c_ref[pl.ds(row0, H), :]
        qrow = q1_ref[pl.ds(row0, H), :]
        w = jnp.zeros((NK, TT), jnp.float32)
        for h in range(H):
            r2 = r2_ref[h * NK:(h + 1) * NK, :]
            e2 = e2_ref[h * NK:(h + 1) * NK, :]
            w = w + jnp.where(r2 < crow[h:h + 1, :], e2 * qrow[h:h + 1, :], 0.0)
        coef.append(_bf(w * gl[g * NK:(g + 1) * NK, :]))
    coef = jnp.concatenate(coef, axis=0)
    acc_ref[...] += _dot(vt_ref[...], coef)

    @pl.when(j == pl.num_programs(1) - 1)
    def _():
        h2 = h1_ref[...] + jnp.transpose(acc_ref[...])
        out_ref[...] = _rms(h2, gf_ref[...])


def _stage_peer(n2t, cc, q1, r2, e2, h1, peer_u, peer_v, final_g):
    N, D = h1.shape
    TT = 512
    ET = 1024
    NE = peer_u.shape[0]
    u_bf = _bf(peer_u)
    vt_bf = _bf(jnp.transpose(peer_v))
    colt = lambda rows: pl.BlockSpec((rows, TT), lambda i, j: (0, i))
    return pl.pallas_call(
        _peer_kernel,
        grid=(N // TT, NE // ET),
        in_specs=[colt(D), colt(NKH), colt(NKH), colt(NKH), colt(NKH),
                  pl.BlockSpec((ET, D), lambda i, j: (j, 0)),
                  pl.BlockSpec((D, ET), lambda i, j: (0, j)),
                  pl.BlockSpec((TT, D), lambda i, j: (i, 0)),
                  pl.BlockSpec((1, D), lambda i, j: (0, 0))],
        out_specs=pl.BlockSpec((TT, D), lambda i, j: (i, 0)),
        out_shape=jax.ShapeDtypeStruct((N, D), jnp.float32),
        scratch_shapes=[pltpu.VMEM((D, TT), jnp.float32)],
        compiler_params=pltpu.CompilerParams(dimension_semantics=("parallel", "arbitrary"),
                                             vmem_limit_bytes=VMEM_LIMIT),
        name="peer",
    )(n2t, cc, q1, r2, e2, u_bf, vt_bf, h1, _row(final_g))


def kernel(x, meta_tokens, norm1_g, w_in, conv_w, conv_b, rg_wa, rg_ba, rg_wx, rg_bx, rg_lambda,
           w_rnn_out, q_norm_g, w_uq, kv_norm_g, w_ukv, w_attn_out, w_out, norm2_g,
           peer_wq, peer_keys1, peer_keys2, peer_u, peer_v, final_g):
    B, S, D = x.shape
    assert w_in.shape[0] == 1, "single layer"
    T = N_META + S
    Tp = -(-T // BLOCK) * BLOCK
    N = B * Tp

    meta = jnp.broadcast_to(meta_tokens[None].astype(x.dtype), (B, N_META, D))
    h0 = jnp.concatenate([meta, x, jnp.zeros((B, Tp - T, D), x.dtype)], axis=1).reshape(N, D)

    xr, gr, grnn, gatt, q, k, v = _stage_proj(h0, Tp, norm1_g[0], w_in[0], q_norm_g[0], w_uq[0],
                                              kv_norm_g[0], w_ukv[0])
    yrg = _stage_rglru(xr, gr, grnn, B, Tp, conv_w[0], conv_b[0], rg_wa[0], rg_ba[0], rg_wx[0], rg_bx[0],
                       rg_lambda[0], w_rnn_out[0])
    o = _stage_attn(q, k, v, B, Tp)
    h1, n2t, cc, q1, r2, e2 = _stage_merge(h0, yrg, o, gatt, w_attn_out[0], w_out[0], norm2_g[0],
                                           peer_wq[0], peer_keys1[0], peer_keys2[0])
    out = _stage_peer(n2t, cc, q1, r2, e2, h1, peer_u[0], peer_v[0], final_g)
    return out.reshape(B, Tp, D)[:, N_META:N_META + S]
```

```python
import functools

import jax
import jax.numpy as jnp
import numpy as np
from jax import lax
from jax.experimental import pallas as pl
from jax.experimental.pallas import tpu as pltpu

N_META = 16
EPS = 1e-6
BLOCK = 128
RG_BLOCKS = 8
CONV_W = 4
RG_C = 8.0
MLA_HEADS = 16
Q_LORA = 384
KV_LORA = 256
QK_NOPE = 64
QK_ROPE = 32
V_DIM = 64
ROPE_BASE = 10000.0
PEER_HEADS = 8
PEER_NKEYS = 128
PEER_TOPK = 16

LANES = 128
SUBLANES = 8
HEAD_PAD = 128
VMEM_LIMIT = 56 * 1024 * 1024
NKH = PEER_NKEYS * PEER_HEADS

_CAND = [(k1, k2) for k1 in range(PEER_TOPK) for k2 in range(PEER_TOPK)
         if (k1 + 1) * (k2 + 1) <= PEER_TOPK]
_NCAND = len(_CAND)
_ROW_START = [min(p for p, (a, _) in enumerate(_CAND) if a == k1) for k1 in range(PEER_TOPK)] + [_NCAND]

BF16_ROWS = 16
_TAKEN = -2.0 ** 100


def _rms(x, g):
    return x * lax.rsqrt(jnp.mean(x * x, axis=-1, keepdims=True) + EPS) * g


def _bf(x):
    return x.astype(jnp.bfloat16)


def _dot(a, b):
    return jnp.dot(a, b, preferred_element_type=jnp.float32)


def _dot_nt(a, b):
    return lax.dot_general(a, b, (((1,), (1,)), ((), ())), preferred_element_type=jnp.float32)


def _const_spec(shape):
    nd = len(shape)
    return pl.BlockSpec(shape, lambda *_: (0,) * nd, pipeline_mode=pl.Buffered(1))


def _row(a):
    return a.reshape(1, -1)


def _proj_kernel(h_ref, g1_ref, w1_ref, w2_ref, qg_ref, kvg_ref, wuq_ref, wuk_ref, wuv_ref,
                 ct_ref, st_ref,
                 xr_ref, gr_ref, grnn_ref, gatt_ref, q_ref, k_ref, v_ref):
    D = h_ref.shape[1]
    n1 = _bf(_rms(h_ref[...], g1_ref[...]))
    for idx, o_ref in enumerate((xr_ref, gr_ref, grnn_ref, gatt_ref)):
        o_ref[...] = _bf(_dot(n1, w1_ref[:, idx * D:(idx + 1) * D]))
    lat = _dot(n1, w2_ref[...])
    cq = lat[:, :Q_LORA]
    ckv = lat[:, Q_LORA:Q_LORA + KV_LORA]
    kr = lat[:, Q_LORA + KV_LORA:]
    ct = ct_ref[...]
    st = st_ref[...]

    def rope(blk):
        return blk * ct + pltpu.roll(blk, HEAD_PAD - QK_ROPE, 1) * st

    nq = _bf(_rms(cq, qg_ref[...]))
    nkv = _bf(_rms(ckv, kvg_ref[...]))
    kr_rot = rope(kr)
    for h in range(MLA_HEADS):
        sl = slice(h * HEAD_PAD, (h + 1) * HEAD_PAD)
        q_ref[:, sl] = _bf(rope(_dot(nq, wuq_ref[:, sl])))
        k_ref[:, sl] = _bf(_dot(nkv, wuk_ref[:, sl]) + kr_rot)
    v_ref[...] = _bf(_dot(nkv, wuv_ref[...]))


def _half_swap(w):
    half = QK_ROPE // 2
    return jnp.concatenate([-w[..., half:], w[..., :half]], axis=-1)


def _stage_proj(h0, Tp, norm1_g, w_in, q_norm_g, w_uq, kv_norm_g, w_ukv):
    N, D = h0.shape
    f32, bf16 = jnp.float32, jnp.bfloat16
    H = MLA_HEADS
    sizes = (D, D, Q_LORA, KV_LORA, QK_ROPE, D, D)
    offs = np.cumsum((0,) + sizes)
    w_xr, w_gr, w_cq, w_ckv, w_kr, w_grnn, w_gatt = [w_in[:, offs[i]:offs[i + 1]] for i in range(7)]
    w1 = _bf(jnp.concatenate([w_xr, w_gr, w_grnn, w_gatt], axis=1))
    w_kr_ext = jnp.concatenate([jnp.zeros((D, QK_NOPE), f32), w_kr, _half_swap(w_kr)], axis=1)
    w2 = _bf(jnp.concatenate([w_cq, w_ckv, w_kr_ext], axis=1))
    wuq = w_uq.reshape(Q_LORA, H, QK_NOPE + QK_ROPE)
    wuq_ext = _bf(jnp.concatenate([wuq, _half_swap(wuq[..., QK_NOPE:])], axis=-1).reshape(Q_LORA, H * HEAD_PAD))
    wukv = w_ukv.reshape(KV_LORA, H, QK_NOPE + V_DIM)
    wuk_ext = _bf(jnp.concatenate([wukv[..., :QK_NOPE], jnp.zeros((KV_LORA, H, HEAD_PAD - QK_NOPE), f32)],
                                  axis=-1).reshape(KV_LORA, H * HEAD_PAD))
    wuv = _bf(wukv[..., QK_NOPE:].reshape(KV_LORA, H * V_DIM))

    half = QK_ROPE // 2
    pos = jnp.arange(Tp, dtype=f32)
    inv = jnp.power(ROPE_BASE, -jnp.arange(half, dtype=f32) * (2.0 / QK_ROPE))
    ang = pos[:, None] * inv[None, :]
    cos, sin = jnp.cos(ang), jnp.sin(ang)
    ctab = jnp.concatenate([jnp.ones((Tp, QK_NOPE), f32), cos, cos, jnp.zeros((Tp, QK_ROPE), f32)], axis=1)
    stab = jnp.concatenate([jnp.zeros((Tp, QK_NOPE), f32), sin, sin, jnp.zeros((Tp, QK_ROPE), f32)], axis=1)

    TM = Tp // 4
    npb = Tp // TM
    tok = lambda width: pl.BlockSpec((TM, width), lambda i: (i, 0))
    return pl.pallas_call(
        _proj_kernel,
        grid=(N // TM,),
        in_specs=[tok(D), _const_spec((1, D)), _const_spec(w1.shape), _const_spec(w2.shape),
                  _const_spec((1, Q_LORA)), _const_spec((1, KV_LORA)),
                  _const_spec(wuq_ext.shape), _const_spec(wuk_ext.shape), _const_spec(wuv.shape),
                  pl.BlockSpec((TM, HEAD_PAD), lambda i: (i % npb, 0)),
                  pl.BlockSpec((TM, HEAD_PAD), lambda i: (i % npb, 0))],
        out_specs=[tok(D), tok(D), tok(D), tok(D), tok(H * HEAD_PAD), tok(H * HEAD_PAD), tok(H * V_DIM)],
        out_shape=[jax.ShapeDtypeStruct((N, D), bf16)] * 4
                  + [jax.ShapeDtypeStruct((N, H * HEAD_PAD), bf16)] * 2
                  + [jax.ShapeDtypeStruct((N, H * V_DIM), bf16)],
        compiler_params=pltpu.CompilerParams(dimension_semantics=("parallel",), vmem_limit_bytes=VMEM_LIMIT),
        name="proj",
    )(h0, _row(norm1_g), w1, w2, _row(q_norm_g), _row(kv_norm_g), wuq_ext, wuk_ext, wuv, ctab, stab)


def _rglru_kernel(xr_ref, gr_ref, grnn_ref, cw_ref, cb_ref, wa_ref, ba_ref, wx_ref, bx_ref,
                  lam_ref, wo_ref, out_ref, xe_ref, hs_ref):
    Tc, D = xr_ref.shape
    c = pl.program_id(1)

    @pl.when(c == 0)
    def _():
        xe_ref[0:SUBLANES, :] = jnp.zeros((SUBLANES, D), jnp.float32)
        hs_ref[...] = jnp.zeros_like(hs_ref)

    x = xr_ref[...].astype(jnp.float32)
    xe_ref[SUBLANES:, :] = x
    y = cb_ref[...] + x * cw_ref[CONV_W - 1:CONV_W, :]
    for kk in range(CONV_W - 1):
        off = SUBLANES - (CONV_W - 1) + kk
        y = y + xe_ref[off:off + Tc, :] * cw_ref[kk:kk + 1, :]
    xe_ref[0:SUBLANES, :] = xe_ref[Tc:Tc + SUBLANES, :]

    yb = _bf(y)
    bs = D // RG_BLOCKS
    r = jnp.concatenate([_dot(yb[:, n * bs:(n + 1) * bs], wa_ref[n]) for n in range(RG_BLOCKS)], axis=1)
    i = jnp.concatenate([_dot(yb[:, n * bs:(n + 1) * bs], wx_ref[n]) for n in range(RG_BLOCKS)], axis=1)
    r = jax.nn.sigmoid(r + ba_ref[...])
    i = jax.nn.sigmoid(i + bx_ref[...])
    lam = lam_ref[...]
    log_sig = jnp.minimum(lam, 0.0) - jnp.log(1.0 + jnp.exp(-jnp.abs(lam)))
    log_a = RG_C * r * log_sig
    a = jnp.exp(log_a)
    u = jnp.sqrt(1.0 - jnp.exp(2.0 * log_a)) * (i * y)

    row = lax.broadcasted_iota(jnp.int32, (Tc, D), 0)
    d = 1
    while d < Tc:
        keep = row >= d
        a_prev = jnp.where(keep, pltpu.roll(a, d, 0), 1.0)
        u_prev = jnp.where(keep, pltpu.roll(u, d, 0), 0.0)
        u = a * u_prev + u
        a = a * a_prev
        d *= 2
    h = u + a * hs_ref[0:1, :]
    hs_ref[...] = jnp.broadcast_to(h[Tc - 1:Tc, :], hs_ref.shape)

    gate = jax.nn.gelu(gr_ref[...].astype(jnp.float32))
    y_rnn = _dot(_bf(h * gate), wo_ref[...])
    out_ref[...] = _bf(jax.nn.sigmoid(grnn_ref[...].astype(jnp.float32)) * y_rnn)


def _stage_rglru(xr, gr, grnn, B, Tp, conv_w, conv_b, rg_wa, rg_ba, rg_wx, rg_bx, rg_lambda, w_rnn_out):
    N, D = xr.shape
    Tc = Tp // 4
    ncb = Tp // Tc
    seq = lambda: pl.BlockSpec((Tc, D), lambda b, c: (b * ncb + c, 0))
    return pl.pallas_call(
        _rglru_kernel,
        grid=(B, ncb),
        in_specs=[seq(), seq(), seq(), _const_spec((CONV_W, D)), _const_spec((1, D)),
                  _const_spec(rg_wa.shape), _const_spec((1, D)),
                  _const_spec(rg_wx.shape), _const_spec((1, D)), _const_spec((1, D)),
                  _const_spec((D, D))],
        out_specs=seq(),
        out_shape=jax.ShapeDtypeStruct((N, D), jnp.bfloat16),
        scratch_shapes=[pltpu.VMEM((Tc + SUBLANES, D), jnp.float32), pltpu.VMEM((SUBLANES, D), jnp.float32)],
        compiler_params=pltpu.CompilerParams(dimension_semantics=("parallel", "arbitrary"),
                                             vmem_limit_bytes=VMEM_LIMIT),
        name="rglru",
    )(xr, gr, grnn, conv_w, _row(conv_b), _bf(rg_wa), _row(rg_ba), _bf(rg_wx), _row(rg_bx),
      _row(rg_lambda), _bf(w_rnn_out))


def _attn_kernel(q_ref, k_ref, v_ref, o_ref, *, q_blocks, scale):
    for j in range(2):
        for (qs, qn) in q_blocks:
            kl = qs + qn
            qh = q_ref[0, qs:qs + qn, j * HEAD_PAD:(j + 1) * HEAD_PAD]
            kh = k_ref[0, 0:kl, j * HEAD_PAD:(j + 1) * HEAD_PAD]
            vh = v_ref[0, 0:kl, j * V_DIM:(j + 1) * V_DIM]
            s = _dot_nt(qh, kh) * scale
            rowi = lax.broadcasted_iota(jnp.int32, (qn, kl), 0) + qs
            coli = lax.broadcasted_iota(jnp.int32, (qn, kl), 1)
            s = jnp.where(coli <= rowi, s, -1e30)
            m = jnp.max(s, axis=-1, keepdims=True)
            p = jnp.exp(s - m)
            l = jnp.sum(p, axis=-1, keepdims=True)
            o = _dot(_bf(p), vh) / l
            o_ref[0, qs:qs + qn, j * V_DIM:(j + 1) * V_DIM] = _bf(o)


def _q_blocks(tp, qb):
    blocks = []
    s = 0
    while s < tp:
        n = min(qb, tp - s)
        blocks.append((s, n))
        s += n
    return tuple(blocks)


def _stage_attn(q, k, v, B, Tp):
    H = MLA_HEADS
    q3 = q.reshape(B, Tp, H * HEAD_PAD)
    k3 = k.reshape(B, Tp, H * HEAD_PAD)
    v3 = v.reshape(B, Tp, H * V_DIM)
    o = pl.pallas_call(
        functools.partial(_attn_kernel, q_blocks=_q_blocks(Tp, 2 * BLOCK),
                          scale=float((QK_NOPE + QK_ROPE) ** -0.5)),
        grid=(B, H // 2),
        in_specs=[pl.BlockSpec((1, Tp, 2 * HEAD_PAD), lambda b, hp: (b, 0, hp)),
                  pl.BlockSpec((1, Tp, 2 * HEAD_PAD), lambda b, hp: (b, 0, hp)),
                  pl.BlockSpec((1, Tp, 2 * V_DIM), lambda b, hp: (b, 0, hp))],
        out_specs=pl.BlockSpec((1, Tp, 2 * V_DIM), lambda b, hp: (b, 0, hp)),
        out_shape=jax.ShapeDtypeStruct((B, Tp, H * V_DIM), jnp.bfloat16),
        compiler_params=pltpu.CompilerParams(dimension_semantics=("parallel", "parallel"),
                                             vmem_limit_bytes=VMEM_LIMIT),
        name="attn",
    )(q3, k3, v3)
    return o.reshape(B * Tp, H * V_DIM)


def _take_topk(entry, nkeys, emit):
    def step(k, carry):
        vals = [entry(key)[...] for key in range(nkeys)]
        idxs = [float(key) for key in range(nkeys)]
        while len(vals) > 1:
            nv, ni = [], []
            for j in range(0, len(vals) - 1, 2):
                right = vals[j + 1] > vals[j]
                nv.append(jnp.where(right, vals[j + 1], vals[j]))
                ni.append(jnp.where(right, idxs[j + 1], idxs[j]))
            if len(vals) % 2:
                nv.append(vals[-1])
                ni.append(idxs[-1])
            vals, idxs = nv, ni
        m, first = vals[0], idxs[0]
        emit(k, m, first)
        mark = _TAKEN * (1.0 + k.astype(jnp.float32) * (1.0 / 32.0))
        for key in range(nkeys):
            ref = entry(key)
            ref[...] = jnp.where(first == float(key), mark, ref[...])
        return carry

    lax.fori_loop(0, PEER_TOPK, step, 0)


def _merge_kernel(h_ref, yr_ref, o_ref, gatt_ref, wao_ref, wout_ref, g2_ref, wq_ref, k1_ref, k2_ref,
                  h1_ref, n2t_ref, c_ref, q1_ref, r2_ref, e2_ref,
                  s_ref, e_ref, r_ref, t_ref, f_ref, cd_ref, t2_ref, f2_ref):
    TM = h_ref.shape[0]
    LG = TM // LANES
    NK = PEER_NKEYS
    H = PEER_HEADS
    f32 = jnp.float32

    y_att = _dot(o_ref[...], wao_ref[...])
    mixed = yr_ref[...].astype(f32) + jax.nn.sigmoid(gatt_ref[...].astype(f32)) * y_att
    h1 = h_ref[...] + _dot(_bf(mixed), wout_ref[...])
    h1_ref[...] = h1
    n2 = _bf(_rms(h1, g2_ref[...]))
    n2t_ref[...] = _bf(jnp.transpose(n2.astype(f32)))
    qp = _bf(_dot(n2, wq_ref[...]))

    for h in range(H):
        for side, kref in enumerate((k1_ref, k2_ref)):
            qs = qp[:, (2 * h + side) * NK:(2 * h + side + 1) * NK]
            st = _dot_nt(kref[...], qs)
            for lg in range(LG):
                s_ref[lg, pl.ds(side * NKH + h, NK, stride=H), :] = st[:, lg * LANES:(lg + 1) * LANES]

    for side in range(2):
        rows = slice(side * NKH, (side + 1) * NKH)
        s0 = s_ref[:, rows, :].reshape(LG, NK, H, LANES)
        e_ref[:, rows, :] = jnp.exp(s0 - jnp.max(s0, axis=1, keepdims=True)).reshape(LG, NKH, LANES)

    def slab(ref, side, key):
        return ref.at[:, side * NKH + key * H:side * NKH + (key + 1) * H, :]

    for side in range(2):
        def emit(k, m, first, side=side):
            t_ref[k, side] = m
            f_ref[k, side] = first
        _take_topk(lambda key, side=side: slab(s_ref, side, key), NK, emit)

    for p, (k1, k2) in enumerate(_CAND):
        cd_ref[p] = t_ref[k1, 0] + t_ref[k2, 1]

    def emit2(k, m, first):
        t2_ref[k] = m
        f2_ref[k] = first
    _take_topk(lambda p: cd_ref.at[p], _NCAND, emit2)

    top = t2_ref[0]
    z = jnp.exp(t2_ref[0] - top)
    for k in range(1, PEER_TOPK):
        z = z + jnp.exp(t2_ref[k] - top)
    inv_z = 1.0 / z
    at_least = []
    for k1 in range(PEER_TOPK):
        n = jnp.zeros((LG, H, LANES), f32)
        for k in range(PEER_TOPK):
            n = n + jnp.where(f2_ref[k] >= float(_ROW_START[k1]), 1.0, 0.0)
        at_least.append(n)
    at_least.append(jnp.zeros((LG, H, LANES), f32))
    counts = [at_least[k1] - at_least[k1 + 1] for k1 in range(PEER_TOPK)]

    for key in range(NK):
        cslab = jnp.zeros((LG, H, LANES), f32)
        for k1 in range(PEER_TOPK):
            cslab = jnp.where(f_ref[k1, 0] == float(key), counts[k1], cslab)
        qslab = slab(e_ref, 0, key)[...] * inv_z
        s2 = slab(s_ref, 1, key)[...]
        rank2 = jnp.where(s2 <= _TAKEN, (s2 * (1.0 / _TAKEN) - 1.0) * 32.0, float(PEER_TOPK))
        r_ref[:, key * H:(key + 1) * H, :] = rank2
        for lg in range(LG):
            cols = slice(lg * LANES, (lg + 1) * LANES)
            c_ref[key * H:(key + 1) * H, cols] = cslab[lg]
            q1_ref[key * H:(key + 1) * H, cols] = qslab[lg]
    for lg in range(LG):
        cols = slice(lg * LANES, (lg + 1) * LANES)
        for h in range(H):
            r2_ref[h * NK:(h + 1) * NK, cols] = _bf(r_ref[lg, pl.ds(h, NK, stride=H), :])
            e2_ref[h * NK:(h + 1) * NK, cols] = _bf(e_ref[lg, pl.ds(NKH + h, NK, stride=H), :])


def _stage_merge(h0, yrg, o, gatt, w_attn_out, w_out, norm2_g, peer_wq, peer_keys1, peer_keys2):
    N, D = h0.shape
    f32 = jnp.float32
    TM = 256
    LG = TM // LANES
    tok = lambda width: pl.BlockSpec((TM, width), lambda i: (i, 0))
    colb = lambda rows: pl.BlockSpec((rows, TM), lambda i: (0, i))
    return pl.pallas_call(
        _merge_kernel,
        grid=(N // TM,),
        in_specs=[tok(D), tok(D), tok(D), tok(D), _const_spec((D, D)), _const_spec((D, D)),
                  _const_spec((1, D)), _const_spec(peer_wq.shape),
                  _const_spec(peer_keys1.shape), _const_spec(peer_keys2.shape)],
        out_specs=[tok(D), colb(D), colb(NKH), colb(NKH), colb(NKH), colb(NKH)],
        out_shape=[jax.ShapeDtypeStruct((N, D), f32), jax.ShapeDtypeStruct((D, N), jnp.bfloat16)]
                  + [jax.ShapeDtypeStruct((NKH, N), f32)] * 2
                  + [jax.ShapeDtypeStruct((NKH, N), jnp.bfloat16)] * 2,
        scratch_shapes=[pltpu.VMEM((LG, 2 * NKH, LANES), f32), pltpu.VMEM((LG, 2 * NKH, LANES), f32),
                        pltpu.VMEM((LG, NKH, LANES), f32),
                        pltpu.VMEM((PEER_TOPK, 2, LG, PEER_HEADS, LANES), f32),
                        pltpu.VMEM((PEER_TOPK, 2, LG, PEER_HEADS, LANES), f32),
                        pltpu.VMEM((_NCAND, LG, PEER_HEADS, LANES), f32),
                        pltpu.VMEM((PEER_TOPK, LG, PEER_HEADS, LANES), f32),
                        pltpu.VMEM((PEER_TOPK, LG, PEER_HEADS, LANES), f32)],
        compiler_params=pltpu.CompilerParams(dimension_semantics=("parallel",), vmem_limit_bytes=VMEM_LIMIT),
        name="merge",
    )(h0, yrg, o, gatt, _bf(w_attn_out), _bf(w_out), _row(norm2_g), _bf(peer_wq),
      _bf(peer_keys1), _bf(peer_keys2))


def _peer_kernel(n2t_ref, c_ref, q1_ref, r2_ref, e2_ref, u_ref, vt_ref, h1_ref, gf_ref,
                 out_ref, acc_ref):
    j = pl.program_id(1)
    ET = u_ref.shape[0]
    TT = n2t_ref.shape[1]
    NK = PEER_NKEYS
    H = PEER_HEADS
    groups = ET // NK

    @pl.when(j == 0)
    def _():
        acc_ref[...] = jnp.zeros_like(acc_ref)

    act = _dot(u_ref[...], n2t_ref[...])
    gl = _bf(jax.nn.gelu(act))
    packed = (NK // BF16_ROWS, BF16_ROWS, TT)
    zero = jnp.zeros(packed, jnp.bfloat16)
    coef = []
    for g in range(groups):
        row0 = pl.multiple_of((j * groups + g) * H, H)
        crow = c_ref[pl.ds(row0, H), :]
        qrow = q1_ref[pl.ds(row0, H), :]
        w = None
        for h in range(H):
            r2 = r2_ref[h * NK:(h + 1) * NK, :].reshape(packed)
            e2 = e2_ref[h * NK:(h + 1) * NK, :].reshape(packed)
            cb = _bf(jnp.broadcast_to(crow[h:h + 1, :], (BF16_ROWS, TT)))
            qb = _bf(jnp.broadcast_to(qrow[h:h + 1, :], (BF16_ROWS, TT)))
            term = jnp.where(r2 < jnp.broadcast_to(cb[None], packed), e2 * qb[None], zero)
            w = term if w is None else w + term
        coef.append(w.reshape(NK, TT) * gl[g * NK:(g + 1) * NK, :])
    coef = jnp.concatenate(coef, axis=0)
    acc_ref[...] += _dot(vt_ref[...], coef)

    @pl.when(j == pl.num_programs(1) - 1)
    def _():
        h2 = h1_ref[...] + jnp.transpose(acc_ref[...])
        out_ref[...] = _rms(h2, gf_ref[...])


def _stage_peer(n2t, cc, q1, r2, e2, h1, peer_u, peer_v, final_g):
    N, D = h1.shape
    TT = 512
    ET = 1024
    NE = peer_u.shape[0]
    u_bf = _bf(peer_u)
    vt_bf = _bf(jnp.transpose(peer_v))
    colt = lambda rows: pl.BlockSpec((rows, TT), lambda i, j: (0, i))
    return pl.pallas_call(
        _peer_kernel,
        grid=(N // TT, NE // ET),
        in_specs=[colt(D), colt(NKH), colt(NKH), colt(NKH), colt(NKH),
                  pl.BlockSpec((ET, D), lambda i, j: (j, 0)),
                  pl.BlockSpec((D, ET), lambda i, j: (0, j)),
                  pl.BlockSpec((TT, D), lambda i, j: (i, 0)),
                  pl.BlockSpec((1, D), lambda i, j: (0, 0))],
        out_specs=pl.BlockSpec((TT, D), lambda i, j: (i, 0)),
        out_shape=jax.ShapeDtypeStruct((N, D), jnp.float32),
        scratch_shapes=[pltpu.VMEM((D, TT), jnp.float32)],
        compiler_params=pltpu.CompilerParams(dimension_semantics=("parallel", "arbitrary"),
                                             vmem_limit_bytes=VMEM_LIMIT),
        name="peer",
    )(n2t, cc, q1, r2, e2, u_bf, vt_bf, h1, _row(final_g))


def kernel(x, meta_tokens, norm1_g, w_in, conv_w, conv_b, rg_wa, rg_ba, rg_wx, rg_bx, rg_lambda,
           w_rnn_out, q_norm_g, w_uq, kv_norm_g, w_ukv, w_attn_out, w_out, norm2_g,
           peer_wq, peer_keys1, peer_keys2, peer_u, peer_v, final_g):
    B, S, D = x.shape
    assert w_in.shape[0] == 1, "single layer"
    T = N_META + S
    Tp = -(-T // BLOCK) * BLOCK
    N = B * Tp

    meta = jnp.broadcast_to(meta_tokens[None].astype(x.dtype), (B, N_META, D))
    h0 = jnp.concatenate([meta, x, jnp.zeros((B, Tp - T, D), x.dtype)], axis=1).reshape(N, D)

    xr, gr, grnn, gatt, q, k, v = _stage_proj(h0, Tp, norm1_g[0], w_in[0], q_norm_g[0], w_uq[0],
                                              kv_norm_g[0], w_ukv[0])
    yrg = _stage_rglru(xr, gr, grnn, B, Tp, conv_w[0], conv_b[0], rg_wa[0], rg_ba[0], rg_wx[0], rg_bx[0],
                       rg_lambda[0], w_rnn_out[0])
    o = _stage_attn(q, k, v, B, Tp)
    h1, n2t, cc, q1, r2, e2 = _stage_merge(h0, yrg, o, gatt, w_attn_out[0], w_out[0], norm2_g[0],
                                           peer_wq[0], peer_keys1[0], peer_keys2[0])
    out = _stage_peer(n2t, cc, q1, r2, e2, h1, peer_u[0], peer_v[0], final_g)
    return out.reshape(B, Tp, D)[:, N_META:N_META + S]
```

```python
import functools

import jax
import jax.numpy as jnp
import numpy as np
from jax import lax
from jax.experimental import pallas as pl
from jax.experimental.pallas import tpu as pltpu

N_META = 16
EPS = 1e-6
BLOCK = 128
RG_BLOCKS = 8
CONV_W = 4
RG_C = 8.0
MLA_HEADS = 16
Q_LORA = 384
KV_LORA = 256
QK_NOPE = 64
QK_ROPE = 32
V_DIM = 64
ROPE_BASE = 10000.0
PEER_HEADS = 8
PEER_NKEYS = 128
PEER_TOPK = 16

LANES = 128
SUBLANES = 8
HEAD_PAD = 128
VMEM_LIMIT = 56 * 1024 * 1024
NKH = PEER_NKEYS * PEER_HEADS

_CAND = [(k1, k2) for k1 in range(PEER_TOPK) for k2 in range(PEER_TOPK)
         if (k1 + 1) * (k2 + 1) <= PEER_TOPK]
_NCAND = len(_CAND)
_ROW_START = [min(p for p, (a, _) in enumerate(_CAND) if a == k1) for k1 in range(PEER_TOPK)] + [_NCAND]

BF16_ROWS = 16
PEER_BLOCK_LANES = 256
_TAKEN = -2.0 ** 100


def _rms(x, g):
    return x * lax.rsqrt(jnp.mean(x * x, axis=-1, keepdims=True) + EPS) * g


def _bf(x):
    return x.astype(jnp.bfloat16)


def _dot(a, b):
    return jnp.dot(a, b, preferred_element_type=jnp.float32)


def _dot_nt(a, b):
    return lax.dot_general(a, b, (((1,), (1,)), ((), ())), preferred_element_type=jnp.float32)


def _const_spec(shape):
    nd = len(shape)
    return pl.BlockSpec(shape, lambda *_: (0,) * nd, pipeline_mode=pl.Buffered(1))


def _row(a):
    return a.reshape(1, -1)


def _proj_kernel(h_ref, g1_ref, w1_ref, w2_ref, qg_ref, kvg_ref, wuq_ref, wuk_ref, wuv_ref,
                 ct_ref, st_ref,
                 xr_ref, gr_ref, grnn_ref, gatt_ref, q_ref, k_ref, v_ref):
    D = h_ref.shape[1]
    n1 = _bf(_rms(h_ref[...], g1_ref[...]))
    for idx, o_ref in enumerate((xr_ref, gr_ref, grnn_ref, gatt_ref)):
        o_ref[...] = _bf(_dot(n1, w1_ref[:, idx * D:(idx + 1) * D]))
    lat = _dot(n1, w2_ref[...])
    cq = lat[:, :Q_LORA]
    ckv = lat[:, Q_LORA:Q_LORA + KV_LORA]
    kr = lat[:, Q_LORA + KV_LORA:]
    ct = ct_ref[...]
    st = st_ref[...]

    def rope(blk):
        return blk * ct + pltpu.roll(blk, HEAD_PAD - QK_ROPE, 1) * st

    nq = _bf(_rms(cq, qg_ref[...]))
    nkv = _bf(_rms(ckv, kvg_ref[...]))
    kr_rot = rope(kr)
    for h in range(MLA_HEADS):
        sl = slice(h * HEAD_PAD, (h + 1) * HEAD_PAD)
        q_ref[:, sl] = _bf(rope(_dot(nq, wuq_ref[:, sl])))
        k_ref[:, sl] = _bf(_dot(nkv, wuk_ref[:, sl]) + kr_rot)
    v_ref[...] = _bf(_dot(nkv, wuv_ref[...]))


def _half_swap(w):
    half = QK_ROPE // 2
    return jnp.concatenate([-w[..., half:], w[..., :half]], axis=-1)


def _stage_proj(h0, Tp, norm1_g, w_in, q_norm_g, w_uq, kv_norm_g, w_ukv):
    N, D = h0.shape
    f32, bf16 = jnp.float32, jnp.bfloat16
    H = MLA_HEADS
    sizes = (D, D, Q_LORA, KV_LORA, QK_ROPE, D, D)
    offs = np.cumsum((0,) + sizes)
    w_xr, w_gr, w_cq, w_ckv, w_kr, w_grnn, w_gatt = [w_in[:, offs[i]:offs[i + 1]] for i in range(7)]
    w1 = _bf(jnp.concatenate([w_xr, w_gr, w_grnn, w_gatt], axis=1))
    w_kr_ext = jnp.concatenate([jnp.zeros((D, QK_NOPE), f32), w_kr, _half_swap(w_kr)], axis=1)
    w2 = _bf(jnp.concatenate([w_cq, w_ckv, w_kr_ext], axis=1))
    wuq = w_uq.reshape(Q_LORA, H, QK_NOPE + QK_ROPE)
    wuq_ext = _bf(jnp.concatenate([wuq, _half_swap(wuq[..., QK_NOPE:])], axis=-1).reshape(Q_LORA, H * HEAD_PAD))
    wukv = w_ukv.reshape(KV_LORA, H, QK_NOPE + V_DIM)
    wuk_ext = _bf(jnp.concatenate([wukv[..., :QK_NOPE], jnp.zeros((KV_LORA, H, HEAD_PAD - QK_NOPE), f32)],
                                  axis=-1).reshape(KV_LORA, H * HEAD_PAD))
    wuv = _bf(wukv[..., QK_NOPE:].reshape(KV_LORA, H * V_DIM))

    half = QK_ROPE // 2
    pos = jnp.arange(Tp, dtype=f32)
    inv = jnp.power(ROPE_BASE, -jnp.arange(half, dtype=f32) * (2.0 / QK_ROPE))
    ang = pos[:, None] * inv[None, :]
    cos, sin = jnp.cos(ang), jnp.sin(ang)
    ctab = jnp.concatenate([jnp.ones((Tp, QK_NOPE), f32), cos, cos, jnp.zeros((Tp, QK_ROPE), f32)], axis=1)
    stab = jnp.concatenate([jnp.zeros((Tp, QK_NOPE), f32), sin, sin, jnp.zeros((Tp, QK_ROPE), f32)], axis=1)

    TM = Tp // 4
    npb = Tp // TM
    tok = lambda width: pl.BlockSpec((TM, width), lambda i: (i, 0))
    return pl.pallas_call(
        _proj_kernel,
        grid=(N // TM,),
        in_specs=[tok(D), _const_spec((1, D)), _const_spec(w1.shape), _const_spec(w2.shape),
                  _const_spec((1, Q_LORA)), _const_spec((1, KV_LORA)),
                  _const_spec(wuq_ext.shape), _const_spec(wuk_ext.shape), _const_spec(wuv.shape),
                  pl.BlockSpec((TM, HEAD_PAD), lambda i: (i % npb, 0)),
                  pl.BlockSpec((TM, HEAD_PAD), lambda i: (i % npb, 0))],
        out_specs=[tok(D), tok(D), tok(D), tok(D), tok(H * HEAD_PAD), tok(H * HEAD_PAD), tok(H * V_DIM)],
        out_shape=[jax.ShapeDtypeStruct((N, D), bf16)] * 4
                  + [jax.ShapeDtypeStruct((N, H * HEAD_PAD), bf16)] * 2
                  + [jax.ShapeDtypeStruct((N, H * V_DIM), bf16)],
        compiler_params=pltpu.CompilerParams(dimension_semantics=("parallel",), vmem_limit_bytes=VMEM_LIMIT),
        name="proj",
    )(h0, _row(norm1_g), w1, w2, _row(q_norm_g), _row(kv_norm_g), wuq_ext, wuk_ext, wuv, ctab, stab)


def _rglru_kernel(xr_ref, gr_ref, grnn_ref, cw_ref, cb_ref, wa_ref, ba_ref, wx_ref, bx_ref,
                  lam_ref, wo_ref, out_ref, xe_ref, hs_ref):
    Tc, D = xr_ref.shape
    c = pl.program_id(1)

    @pl.when(c == 0)
    def _():
        xe_ref[0:SUBLANES, :] = jnp.zeros((SUBLANES, D), jnp.float32)
        hs_ref[...] = jnp.zeros_like(hs_ref)

    x = xr_ref[...].astype(jnp.float32)
    xe_ref[SUBLANES:, :] = x
    y = cb_ref[...] + x * cw_ref[CONV_W - 1:CONV_W, :]
    for kk in range(CONV_W - 1):
        off = SUBLANES - (CONV_W - 1) + kk
        y = y + xe_ref[off:off + Tc, :] * cw_ref[kk:kk + 1, :]
    xe_ref[0:SUBLANES, :] = xe_ref[Tc:Tc + SUBLANES, :]

    yb = _bf(y)
    bs = D // RG_BLOCKS
    r = jnp.concatenate([_dot(yb[:, n * bs:(n + 1) * bs], wa_ref[n]) for n in range(RG_BLOCKS)], axis=1)
    i = jnp.concatenate([_dot(yb[:, n * bs:(n + 1) * bs], wx_ref[n]) for n in range(RG_BLOCKS)], axis=1)
    r = jax.nn.sigmoid(r + ba_ref[...])
    i = jax.nn.sigmoid(i + bx_ref[...])
    lam = lam_ref[...]
    log_sig = jnp.minimum(lam, 0.0) - jnp.log(1.0 + jnp.exp(-jnp.abs(lam)))
    log_a = RG_C * r * log_sig
    a = jnp.exp(log_a)
    u = jnp.sqrt(1.0 - jnp.exp(2.0 * log_a)) * (i * y)

    row = lax.broadcasted_iota(jnp.int32, (Tc, D), 0)
    d = 1
    while d < Tc:
        keep = row >= d
        a_prev = jnp.where(keep, pltpu.roll(a, d, 0), 1.0)
        u_prev = jnp.where(keep, pltpu.roll(u, d, 0), 0.0)
        u = a * u_prev + u
        a = a * a_prev
        d *= 2
    h = u + a * hs_ref[0:1, :]
    hs_ref[...] = jnp.broadcast_to(h[Tc - 1:Tc, :], hs_ref.shape)

    gate = jax.nn.gelu(gr_ref[...].astype(jnp.float32))
    y_rnn = _dot(_bf(h * gate), wo_ref[...])
    out_ref[...] = _bf(jax.nn.sigmoid(grnn_ref[...].astype(jnp.float32)) * y_rnn)


def _stage_rglru(xr, gr, grnn, B, Tp, conv_w, conv_b, rg_wa, rg_ba, rg_wx, rg_bx, rg_lambda, w_rnn_out):
    N, D = xr.shape
    Tc = Tp // 4
    ncb = Tp // Tc
    seq = lambda: pl.BlockSpec((Tc, D), lambda b, c: (b * ncb + c, 0))
    return pl.pallas_call(
        _rglru_kernel,
        grid=(B, ncb),
        in_specs=[seq(), seq(), seq(), _const_spec((CONV_W, D)), _const_spec((1, D)),
                  _const_spec(rg_wa.shape), _const_spec((1, D)),
                  _const_spec(rg_wx.shape), _const_spec((1, D)), _const_spec((1, D)),
                  _const_spec((D, D))],
        out_specs=seq(),
        out_shape=jax.ShapeDtypeStruct((N, D), jnp.bfloat16),
        scratch_shapes=[pltpu.VMEM((Tc + SUBLANES, D), jnp.float32), pltpu.VMEM((SUBLANES, D), jnp.float32)],
        compiler_params=pltpu.CompilerParams(dimension_semantics=("parallel", "arbitrary"),
                                             vmem_limit_bytes=VMEM_LIMIT),
        name="rglru",
    )(xr, gr, grnn, conv_w, _row(conv_b), _bf(rg_wa), _row(rg_ba), _bf(rg_wx), _row(rg_bx),
      _row(rg_lambda), _bf(w_rnn_out))


def _attn_kernel(q_ref, k_ref, v_ref, o_ref, *, q_blocks, scale):
    for j in range(2):
        for (qs, qn) in q_blocks:
            kl = qs + qn
            qh = q_ref[0, qs:qs + qn, j * HEAD_PAD:(j + 1) * HEAD_PAD]
            kh = k_ref[0, 0:kl, j * HEAD_PAD:(j + 1) * HEAD_PAD]
            vh = v_ref[0, 0:kl, j * V_DIM:(j + 1) * V_DIM]
            s = _dot_nt(qh, kh) * scale
            rowi = lax.broadcasted_iota(jnp.int32, (qn, kl), 0) + qs
            coli = lax.broadcasted_iota(jnp.int32, (qn, kl), 1)
            s = jnp.where(coli <= rowi, s, -1e30)
            m = jnp.max(s, axis=-1, keepdims=True)
            p = jnp.exp(s - m)
            l = jnp.sum(p, axis=-1, keepdims=True)
            o = _dot(_bf(p), vh) / l
            o_ref[0, qs:qs + qn, j * V_DIM:(j + 1) * V_DIM] = _bf(o)


def _q_blocks(tp, qb):
    blocks = []
    s = 0
    while s < tp:
        n = min(qb, tp - s)
        blocks.append((s, n))
        s += n
    return tuple(blocks)


def _stage_attn(q, k, v, B, Tp):
    H = MLA_HEADS
    q3 = q.reshape(B, Tp, H * HEAD_PAD)
    k3 = k.reshape(B, Tp, H * HEAD_PAD)
    v3 = v.reshape(B, Tp, H * V_DIM)
    o = pl.pallas_call(
        functools.partial(_attn_kernel, q_blocks=_q_blocks(Tp, 2 * BLOCK),
                          scale=float((QK_NOPE + QK_ROPE) ** -0.5)),
        grid=(B, H // 2),
        in_specs=[pl.BlockSpec((1, Tp, 2 * HEAD_PAD), lambda b, hp: (b, 0, hp)),
                  pl.BlockSpec((1, Tp, 2 * HEAD_PAD), lambda b, hp: (b, 0, hp)),
                  pl.BlockSpec((1, Tp, 2 * V_DIM), lambda b, hp: (b, 0, hp))],
        out_specs=pl.BlockSpec((1, Tp, 2 * V_DIM), lambda b, hp: (b, 0, hp)),
        out_shape=jax.ShapeDtypeStruct((B, Tp, H * V_DIM), jnp.bfloat16),
        compiler_params=pltpu.CompilerParams(dimension_semantics=("parallel", "parallel"),
                                             vmem_limit_bytes=VMEM_LIMIT),
        name="attn",
    )(q3, k3, v3)
    return o.reshape(B * Tp, H * V_DIM)


def _take_topk(entry, nkeys, emit):
    def step(k, carry):
        vals = [entry(key)[...] for key in range(nkeys)]
        idxs = [float(key) for key in range(nkeys)]
        while len(vals) > 1:
            nv, ni = [], []
            for j in range(0, len(vals) - 1, 2):
                right = vals[j + 1] > vals[j]
                nv.append(jnp.where(right, vals[j + 1], vals[j]))
                ni.append(jnp.where(right, idxs[j + 1], idxs[j]))
            if len(vals) % 2:
                nv.append(vals[-1])
                ni.append(idxs[-1])
            vals, idxs = nv, ni
        m, first = vals[0], idxs[0]
        emit(k, m, first)
        mark = _TAKEN * (1.0 + jnp.asarray(k, jnp.float32) * (1.0 / 32.0))
        for key in range(nkeys):
            ref = entry(key)
            ref[...] = jnp.where(first == float(key), mark, ref[...])
        return carry

    lax.fori_loop(0, PEER_TOPK, step, 0)


def _merge_kernel(h_ref, yr_ref, o_ref, gatt_ref, wao_ref, wout_ref, g2_ref, wq_ref, k1_ref, k2_ref,
                  h1_ref, n2t_ref, c_ref, q1_ref, r2_ref, e2_ref,
                  s_ref, e_ref, r_ref, t_ref, f_ref, cd_ref, t2_ref, f2_ref):
    TM = h_ref.shape[0]
    LG = TM // LANES
    NK = PEER_NKEYS
    H = PEER_HEADS
    f32 = jnp.float32

    y_att = _dot(o_ref[...], wao_ref[...])
    mixed = yr_ref[...].astype(f32) + jax.nn.sigmoid(gatt_ref[...].astype(f32)) * y_att
    h1 = h_ref[...] + _dot(_bf(mixed), wout_ref[...])
    h1_ref[...] = h1
    n2 = _bf(_rms(h1, g2_ref[...]))
    n2t_ref[...] = _bf(jnp.transpose(n2.astype(f32)))
    qp = _bf(_dot(n2, wq_ref[...]))

    for h in range(H):
        for side, kref in enumerate((k1_ref, k2_ref)):
            qs = qp[:, (2 * h + side) * NK:(2 * h + side + 1) * NK]
            st = _dot_nt(kref[...], qs)
            for lg in range(LG):
                s_ref[lg, pl.ds(side * NKH + h, NK, stride=H), :] = st[:, lg * LANES:(lg + 1) * LANES]

    for side in range(2):
        rows = slice(side * NKH, (side + 1) * NKH)
        s0 = s_ref[:, rows, :].reshape(LG, NK, H, LANES)
        e_ref[:, rows, :] = jnp.exp(s0 - jnp.max(s0, axis=1, keepdims=True)).reshape(LG, NKH, LANES)

    def slab(ref, side, key):
        return ref.at[:, side * NKH + key * H:side * NKH + (key + 1) * H, :]

    for side in range(2):
        def emit(k, m, first, side=side):
            t_ref[k, side] = m
            f_ref[k, side] = first
        _take_topk(lambda key, side=side: slab(s_ref, side, key), NK, emit)

    for p, (k1, k2) in enumerate(_CAND):
        cd_ref[p] = t_ref[k1, 0] + t_ref[k2, 1]

    def emit2(k, m, first):
        t2_ref[k] = m
        f2_ref[k] = first
    _take_topk(lambda p: cd_ref.at[p], _NCAND, emit2)

    top = t2_ref[0]
    z = jnp.exp(t2_ref[0] - top)
    for k in range(1, PEER_TOPK):
        z = z + jnp.exp(t2_ref[k] - top)
    inv_z = 1.0 / z
    at_least = []
    for k1 in range(PEER_TOPK):
        n = jnp.zeros((LG, H, LANES), f32)
        for k in range(PEER_TOPK):
            n = n + jnp.where(f2_ref[k] >= float(_ROW_START[k1]), 1.0, 0.0)
        at_least.append(n)
    at_least.append(jnp.zeros((LG, H, LANES), f32))
    counts = [at_least[k1] - at_least[k1 + 1] for k1 in range(PEER_TOPK)]

    for key in range(NK):
        cslab = jnp.zeros((LG, H, LANES), f32)
        for k1 in range(PEER_TOPK):
            cslab = jnp.where(f_ref[k1, 0] == float(key), counts[k1], cslab)
        qslab = slab(e_ref, 0, key)[...] * inv_z
        s2 = slab(s_ref, 1, key)[...]
        rank2 = jnp.where(s2 <= _TAKEN, (s2 * (1.0 / _TAKEN) - 1.0) * 32.0, float(PEER_TOPK))
        r_ref[:, key * H:(key + 1) * H, :] = rank2
        for lg in range(LG):
            cols = slice(lg * LANES, (lg + 1) * LANES)
            c_ref[key * H:(key + 1) * H, cols] = cslab[lg]
            q1_ref[key * H:(key + 1) * H, cols] = qslab[lg]
    for lg in range(LG):
        cols = slice(lg * LANES, (lg + 1) * LANES)
        for h in range(H):
            r2_ref[h * NK:(h + 1) * NK, cols] = _bf(r_ref[lg, pl.ds(h, NK, stride=H), :])
            e2_ref[h * NK:(h + 1) * NK, cols] = _bf(e_ref[lg, pl.ds(NKH + h, NK, stride=H), :])


def _stage_merge(h0, yrg, o, gatt, w_attn_out, w_out, norm2_g, peer_wq, peer_keys1, peer_keys2):
    N, D = h0.shape
    f32 = jnp.float32
    TM = 256
    LG = TM // LANES
    tok = lambda width: pl.BlockSpec((TM, width), lambda i: (i, 0))
    colb = lambda rows: pl.BlockSpec((rows, TM), lambda i: (0, i))
    return pl.pallas_call(
        _merge_kernel,
        grid=(N // TM,),
        in_specs=[tok(D), tok(D), tok(D), tok(D), _const_spec((D, D)), _const_spec((D, D)),
                  _const_spec((1, D)), _const_spec(peer_wq.shape),
                  _const_spec(peer_keys1.shape), _const_spec(peer_keys2.shape)],
        out_specs=[tok(D), colb(D), colb(NKH), colb(NKH), colb(NKH), colb(NKH)],
        out_shape=[jax.ShapeDtypeStruct((N, D), f32), jax.ShapeDtypeStruct((D, N), jnp.bfloat16)]
                  + [jax.ShapeDtypeStruct((NKH, N), f32)] * 2
                  + [jax.ShapeDtypeStruct((NKH, N), jnp.bfloat16)] * 2,
        scratch_shapes=[pltpu.VMEM((LG, 2 * NKH, LANES), f32), pltpu.VMEM((LG, 2 * NKH, LANES), f32),
                        pltpu.VMEM((LG, NKH, LANES), f32),
                        pltpu.VMEM((PEER_TOPK, 2, LG, PEER_HEADS, LANES), f32),
                        pltpu.VMEM((PEER_TOPK, 2, LG, PEER_HEADS, LANES), f32),
                        pltpu.VMEM((_NCAND, LG, PEER_HEADS, LANES), f32),
                        pltpu.VMEM((PEER_TOPK, LG, PEER_HEADS, LANES), f32),
                        pltpu.VMEM((PEER_TOPK, LG, PEER_HEADS, LANES), f32)],
        compiler_params=pltpu.CompilerParams(dimension_semantics=("parallel",), vmem_limit_bytes=VMEM_LIMIT),
        name="merge",
    )(h0, yrg, o, gatt, _bf(w_attn_out), _bf(w_out), _row(norm2_g), _bf(peer_wq),
      _bf(peer_keys1), _bf(peer_keys2))


def _gelu_tanh(x):
    c = 0.7978845608028654
    hx = 0.5 * x
    return hx + hx * jnp.tanh(x * (c + (c * 0.044715) * (x * x)))


def _peer_kernel(n2t_ref, c_ref, q1_ref, r2_ref, e2_ref, u_ref, vt_ref, h1_ref, gf_ref,
                 out_ref, acc_ref, act_ref, coef_ref):
    j = pl.program_id(1)
    ET = u_ref.shape[0]
    TT = n2t_ref.shape[1]
    NK = PEER_NKEYS
    H = PEER_HEADS
    groups = ET // NK

    @pl.when(j == 0)
    def _():
        acc_ref[...] = jnp.zeros_like(acc_ref)

    act_ref[...] = _bf(_dot(u_ref[...], n2t_ref[...]))
    BW = PEER_BLOCK_LANES
    packed = (NK // BF16_ROWS, BF16_ROWS, BW)
    zero = jnp.zeros(packed, jnp.bfloat16)
    for g in range(groups):
        crow = c_ref[g * H:(g + 1) * H, :]
        qrow = q1_ref[g * H:(g + 1) * H, :]
        for lb in range(TT // BW):
            cols = slice(lb * BW, (lb + 1) * BW)
            w = None
            for h in range(H):
                r2 = r2_ref[h * NK:(h + 1) * NK, cols].reshape(packed)
                e2 = e2_ref[h * NK:(h + 1) * NK, cols].reshape(packed)
                cb = _bf(jnp.broadcast_to(crow[h:h + 1, cols], (BF16_ROWS, BW)))
                qb = _bf(jnp.broadcast_to(qrow[h:h + 1, cols], (BF16_ROWS, BW)))
                term = jnp.where(r2 < jnp.broadcast_to(cb[None], packed), e2 * qb[None], zero)
                w = term if w is None else w + term
            rows = slice(g * NK, (g + 1) * NK)
            coef_ref[rows, cols] = w.reshape(NK, BW) * _gelu_tanh(act_ref[rows, cols])
    acc_ref[...] += _dot(vt_ref[...], coef_ref[...])

    @pl.when(j == pl.num_programs(1) - 1)
    def _():
        h2 = h1_ref[...] + jnp.transpose(acc_ref[...])
        out_ref[...] = _rms(h2, gf_ref[...])


def _stage_peer(n2t, cc, q1, r2, e2, h1, peer_u, peer_v, final_g):
    N, D = h1.shape
    TT = 1024
    ET = 1024
    NE = peer_u.shape[0]
    u_bf = _bf(peer_u)
    vt_bf = _bf(jnp.transpose(peer_v))
    colt = lambda rows: pl.BlockSpec((rows, TT), lambda i, j: (0, i))
    grp = pl.BlockSpec((ET // PEER_NKEYS * PEER_HEADS, TT), lambda i, j: (j, i))
    return pl.pallas_call(
        _peer_kernel,
        grid=(N // TT, NE // ET),
        in_specs=[colt(D), grp, grp, colt(NKH), colt(NKH),
                  pl.BlockSpec((ET, D), lambda i, j: (j, 0)),
                  pl.BlockSpec((D, ET), lambda i, j: (0, j)),
                  pl.BlockSpec((TT, D), lambda i, j: (i, 0), pipeline_mode=pl.Buffered(1)),
                  pl.BlockSpec((1, D), lambda i, j: (0, 0))],
        out_specs=pl.BlockSpec((TT, D), lambda i, j: (i, 0)),
        out_shape=jax.ShapeDtypeStruct((N, D), jnp.float32),
        scratch_shapes=[pltpu.VMEM((D, TT), jnp.float32), pltpu.VMEM((ET, TT), jnp.bfloat16),
                        pltpu.VMEM((ET, TT), jnp.bfloat16)],
        compiler_params=pltpu.CompilerParams(dimension_semantics=("parallel", "arbitrary"),
                                             vmem_limit_bytes=VMEM_LIMIT),
        name="peer",
    )(n2t, cc, q1, r2, e2, u_bf, vt_bf, h1, _row(final_g))


def kernel(x, meta_tokens, norm1_g, w_in, conv_w, conv_b, rg_wa, rg_ba, rg_wx, rg_bx, rg_lambda,
           w_rnn_out, q_norm_g, w_uq, kv_norm_g, w_ukv, w_attn_out, w_out, norm2_g,
           peer_wq, peer_keys1, peer_keys2, peer_u, peer_v, final_g):
    B, S, D = x.shape
    assert w_in.shape[0] == 1, "single layer"
    T = N_META + S
    Tp = -(-T // BLOCK) * BLOCK
    N = B * Tp

    meta = jnp.broadcast_to(meta_tokens[None].astype(x.dtype), (B, N_META, D))
    h0 = jnp.concatenate([meta, x, jnp.zeros((B, Tp - T, D), x.dtype)], axis=1).reshape(N, D)

    xr, gr, grnn, gatt, q, k, v = _stage_proj(h0, Tp, norm1_g[0], w_in[0], q_norm_g[0], w_uq[0],
                                              kv_norm_g[0], w_ukv[0])
    yrg = _stage_rglru(xr, gr, grnn, B, Tp, conv_w[0], conv_b[0], rg_wa[0], rg_ba[0], rg_wx[0], rg_bx[0],
                       rg_lambda[0], w_rnn_out[0])
    o = _stage_attn(q, k, v, B, Tp)
    h1, n2t, cc, q1, r2, e2 = _stage_merge(h0, yrg, o, gatt, w_attn_out[0], w_out[0], norm2_g[0],
                                           peer_wq[0], peer_keys1[0], peer_keys2[0])
    out = _stage_peer(n2t, cc, q1, r2, e2, h1, peer_u[0], peer_v[0], final_g)
    return out.reshape(B, Tp, D)[:, N_META:N_META + S]
```

```python
import functools

import jax
import jax.numpy as jnp
import numpy as np
from jax import lax
from jax.experimental import pallas as pl
from jax.experimental.pallas import tpu as pltpu

N_META = 16
EPS = 1e-6
BLOCK = 128
RG_BLOCKS = 8
CONV_W = 4
RG_C = 8.0
MLA_HEADS = 16
Q_LORA = 384
KV_LORA = 256
QK_NOPE = 64
QK_ROPE = 32
V_DIM = 64
ROPE_BASE = 10000.0
PEER_HEADS = 8
PEER_NKEYS = 128
PEER_TOPK = 16

LANES = 128
SUBLANES = 8
HEAD_PAD = 128
VMEM_LIMIT = 56 * 1024 * 1024
NKH = PEER_NKEYS * PEER_HEADS

_CAND = [(k1, k2) for k1 in range(PEER_TOPK) for k2 in range(PEER_TOPK)
         if (k1 + 1) * (k2 + 1) <= PEER_TOPK]
_NCAND = len(_CAND)
_ROW_START = [min(p for p, (a, _) in enumerate(_CAND) if a == k1) for k1 in range(PEER_TOPK)] + [_NCAND]

ATTN_QBLOCK = 512
ATTN_ROWS = 32
Q_PRESCALE = float((QK_NOPE + QK_ROPE) ** -0.5 * np.log2(np.e))
BF16_ROWS = 16
PEER_BLOCK_LANES = 256
_TAKEN = -2.0 ** 100


def _rms(x, g):
    return x * lax.rsqrt(jnp.mean(x * x, axis=-1, keepdims=True) + EPS) * g


def _bf(x):
    return x.astype(jnp.bfloat16)


def _dot(a, b):
    return jnp.dot(a, b, preferred_element_type=jnp.float32)


def _dot_nt(a, b):
    return lax.dot_general(a, b, (((1,), (1,)), ((), ())), preferred_element_type=jnp.float32)


def _const_spec(shape):
    nd = len(shape)
    return pl.BlockSpec(shape, lambda *_: (0,) * nd, pipeline_mode=pl.Buffered(1))


def _row(a):
    return a.reshape(1, -1)


def _proj_kernel(h_ref, g1_ref, w1_ref, w2_ref, qg_ref, kvg_ref, wuq_ref, wuk_ref, wuv_ref,
                 ct_ref, st_ref,
                 xr_ref, gr_ref, grnn_ref, gatt_ref, q_ref, k_ref, v_ref):
    D = h_ref.shape[1]
    n1 = _bf(_rms(h_ref[...], g1_ref[...]))
    for idx, o_ref in enumerate((xr_ref, gr_ref, grnn_ref, gatt_ref)):
        o_ref[...] = _bf(_dot(n1, w1_ref[:, idx * D:(idx + 1) * D]))
    lat = _dot(n1, w2_ref[...])
    cq = lat[:, :Q_LORA]
    ckv = lat[:, Q_LORA:Q_LORA + KV_LORA]
    kr = lat[:, Q_LORA + KV_LORA:]
    ct = ct_ref[...]
    st = st_ref[...]

    def rope(blk):
        return blk * ct + pltpu.roll(blk, HEAD_PAD - QK_ROPE, 1) * st

    nq = _bf(_rms(cq, qg_ref[...]))
    nkv = _bf(_rms(ckv, kvg_ref[...]))
    kr_rot = rope(kr)
    for h in range(MLA_HEADS):
        sl = slice(h * HEAD_PAD, (h + 1) * HEAD_PAD)
        q_ref[:, sl] = _bf(rope(_dot(nq, wuq_ref[:, sl])) * Q_PRESCALE)
        k_ref[:, sl] = _bf(_dot(nkv, wuk_ref[:, sl]) + kr_rot)
    v_ref[...] = _bf(_dot(nkv, wuv_ref[...]))


def _half_swap(w):
    half = QK_ROPE // 2
    return jnp.concatenate([-w[..., half:], w[..., :half]], axis=-1)


def _stage_proj(h0, Tp, norm1_g, w_in, q_norm_g, w_uq, kv_norm_g, w_ukv):
    N, D = h0.shape
    f32, bf16 = jnp.float32, jnp.bfloat16
    H = MLA_HEADS
    sizes = (D, D, Q_LORA, KV_LORA, QK_ROPE, D, D)
    offs = np.cumsum((0,) + sizes)
    w_xr, w_gr, w_cq, w_ckv, w_kr, w_grnn, w_gatt = [w_in[:, offs[i]:offs[i + 1]] for i in range(7)]
    w1 = _bf(jnp.concatenate([w_xr, w_gr, w_grnn, w_gatt], axis=1))
    w_kr_ext = jnp.concatenate([jnp.zeros((D, QK_NOPE), f32), w_kr, _half_swap(w_kr)], axis=1)
    w2 = _bf(jnp.concatenate([w_cq, w_ckv, w_kr_ext], axis=1))
    wuq = w_uq.reshape(Q_LORA, H, QK_NOPE + QK_ROPE)
    wuq_ext = _bf(jnp.concatenate([wuq, _half_swap(wuq[..., QK_NOPE:])], axis=-1).reshape(Q_LORA, H * HEAD_PAD))
    wukv = w_ukv.reshape(KV_LORA, H, QK_NOPE + V_DIM)
    wuk_ext = _bf(jnp.concatenate([wukv[..., :QK_NOPE], jnp.zeros((KV_LORA, H, HEAD_PAD - QK_NOPE), f32)],
                                  axis=-1).reshape(KV_LORA, H * HEAD_PAD))
    wuv = _bf(wukv[..., QK_NOPE:].reshape(KV_LORA, H * V_DIM))

    half = QK_ROPE // 2
    pos = jnp.arange(Tp, dtype=f32)
    inv = jnp.power(ROPE_BASE, -jnp.arange(half, dtype=f32) * (2.0 / QK_ROPE))
    ang = pos[:, None] * inv[None, :]
    cos, sin = jnp.cos(ang), jnp.sin(ang)
    ctab = jnp.concatenate([jnp.ones((Tp, QK_NOPE), f32), cos, cos, jnp.zeros((Tp, QK_ROPE), f32)], axis=1)
    stab = jnp.concatenate([jnp.zeros((Tp, QK_NOPE), f32), sin, sin, jnp.zeros((Tp, QK_ROPE), f32)], axis=1)

    TM = Tp // 4
    npb = Tp // TM
    tok = lambda width: pl.BlockSpec((TM, width), lambda i: (i, 0))
    return pl.pallas_call(
        _proj_kernel,
        grid=(N // TM,),
        in_specs=[tok(D), _const_spec((1, D)), _const_spec(w1.shape), _const_spec(w2.shape),
                  _const_spec((1, Q_LORA)), _const_spec((1, KV_LORA)),
                  _const_spec(wuq_ext.shape), _const_spec(wuk_ext.shape), _const_spec(wuv.shape),
                  pl.BlockSpec((TM, HEAD_PAD), lambda i: (i % npb, 0)),
                  pl.BlockSpec((TM, HEAD_PAD), lambda i: (i % npb, 0))],
        out_specs=[tok(D), tok(D), tok(D), tok(D), tok(H * HEAD_PAD), tok(H * HEAD_PAD), tok(H * V_DIM)],
        out_shape=[jax.ShapeDtypeStruct((N, D), bf16)] * 4
                  + [jax.ShapeDtypeStruct((N, H * HEAD_PAD), bf16)] * 2
                  + [jax.ShapeDtypeStruct((N, H * V_DIM), bf16)],
        compiler_params=pltpu.CompilerParams(dimension_semantics=("parallel",), vmem_limit_bytes=VMEM_LIMIT),
        name="proj",
    )(h0, _row(norm1_g), w1, w2, _row(q_norm_g), _row(kv_norm_g), wuq_ext, wuk_ext, wuv, ctab, stab)


def _rglru_kernel(xr_ref, gr_ref, grnn_ref, cw_ref, cb_ref, wa_ref, ba_ref, wx_ref, bx_ref,
                  lam_ref, wo_ref, out_ref, xe_ref, hs_ref):
    Tc, D = xr_ref.shape
    c = pl.program_id(1)

    @pl.when(c == 0)
    def _():
        xe_ref[0:SUBLANES, :] = jnp.zeros((SUBLANES, D), jnp.float32)
        hs_ref[...] = jnp.zeros_like(hs_ref)

    x = xr_ref[...].astype(jnp.float32)
    xe_ref[SUBLANES:, :] = x
    y = cb_ref[...] + x * cw_ref[CONV_W - 1:CONV_W, :]
    for kk in range(CONV_W - 1):
        off = SUBLANES - (CONV_W - 1) + kk
        y = y + xe_ref[off:off + Tc, :] * cw_ref[kk:kk + 1, :]
    xe_ref[0:SUBLANES, :] = xe_ref[Tc:Tc + SUBLANES, :]

    yb = _bf(y)
    bs = D // RG_BLOCKS
    r = jnp.concatenate([_dot(yb[:, n * bs:(n + 1) * bs], wa_ref[n]) for n in range(RG_BLOCKS)], axis=1)
    i = jnp.concatenate([_dot(yb[:, n * bs:(n + 1) * bs], wx_ref[n]) for n in range(RG_BLOCKS)], axis=1)
    r = jax.nn.sigmoid(r + ba_ref[...])
    i = jax.nn.sigmoid(i + bx_ref[...])
    lam = lam_ref[...]
    log_sig = jnp.minimum(lam, 0.0) - jnp.log(1.0 + jnp.exp(-jnp.abs(lam)))
    log_a = RG_C * r * log_sig
    a = jnp.exp(log_a)
    u = jnp.sqrt(1.0 - jnp.exp(2.0 * log_a)) * (i * y)

    row = lax.broadcasted_iota(jnp.int32, (Tc, D), 0)
    d = 1
    while d < Tc:
        keep = row >= d
        a_prev = jnp.where(keep, pltpu.roll(a, d, 0), 1.0)
        u_prev = jnp.where(keep, pltpu.roll(u, d, 0), 0.0)
        u = a * u_prev + u
        a = a * a_prev
        d *= 2
    h = u + a * hs_ref[0:1, :]
    hs_ref[...] = jnp.broadcast_to(h[Tc - 1:Tc, :], hs_ref.shape)

    gate = jax.nn.gelu(gr_ref[...].astype(jnp.float32))
    y_rnn = _dot(_bf(h * gate), wo_ref[...])
    out_ref[...] = _bf(jax.nn.sigmoid(grnn_ref[...].astype(jnp.float32)) * y_rnn)


def _stage_rglru(xr, gr, grnn, B, Tp, conv_w, conv_b, rg_wa, rg_ba, rg_wx, rg_bx, rg_lambda, w_rnn_out):
    N, D = xr.shape
    Tc = Tp // 4
    ncb = Tp // Tc
    seq = lambda: pl.BlockSpec((Tc, D), lambda b, c: (b * ncb + c, 0))
    return pl.pallas_call(
        _rglru_kernel,
        grid=(B, ncb),
        in_specs=[seq(), seq(), seq(), _const_spec((CONV_W, D)), _const_spec((1, D)),
                  _const_spec(rg_wa.shape), _const_spec((1, D)),
                  _const_spec(rg_wx.shape), _const_spec((1, D)), _const_spec((1, D)),
                  _const_spec((D, D))],
        out_specs=seq(),
        out_shape=jax.ShapeDtypeStruct((N, D), jnp.bfloat16),
        scratch_shapes=[pltpu.VMEM((Tc + SUBLANES, D), jnp.float32), pltpu.VMEM((SUBLANES, D), jnp.float32)],
        compiler_params=pltpu.CompilerParams(dimension_semantics=("parallel", "arbitrary"),
                                             vmem_limit_bytes=VMEM_LIMIT),
        name="rglru",
    )(xr, gr, grnn, conv_w, _row(conv_b), _bf(rg_wa), _row(rg_ba), _bf(rg_wx), _row(rg_bx),
      _row(rg_lambda), _bf(w_rnn_out))


def _attn_kernel(q_ref, k_ref, v_ref, o_ref, s_ref, p_ref, vt_ref, ot_ref, *, q_blocks):
    f32 = jnp.float32
    vt_ref[...] = _bf(jnp.transpose(v_ref[0].astype(f32)))
    for (qs, qn) in q_blocks:
        kl = qs + qn
        for j in range(2):
            sj, pj = s_ref.at[j], p_ref.at[j]
            qh = q_ref[0, qs:qs + qn, j * HEAD_PAD:(j + 1) * HEAD_PAD]
            kh = k_ref[0, 0:kl, j * HEAD_PAD:(j + 1) * HEAD_PAD]
            sj[0:kl, 0:qn] = _dot_nt(kh, qh)
            key = lax.broadcasted_iota(jnp.int32, (qn, qn), 0)
            qry = lax.broadcasted_iota(jnp.int32, (qn, qn), 1)
            sj[qs:kl, 0:qn] = jnp.where(key <= qry, sj[qs:kl, 0:qn], -1e30)
            m8 = jnp.full((SUBLANES, qn), -jnp.inf, f32)
            for r in range(0, kl, ATTN_ROWS):
                blk = sj[r:r + ATTN_ROWS, 0:qn].reshape(ATTN_ROWS // SUBLANES, SUBLANES, qn)
                m8 = jnp.maximum(m8, jnp.max(blk, axis=0))
            m = jnp.max(m8, axis=0, keepdims=True)
            l8 = jnp.zeros((SUBLANES, qn), f32)
            for r in range(0, kl, ATTN_ROWS):
                p = jnp.exp2(sj[r:r + ATTN_ROWS, 0:qn] - m)
                l8 = l8 + jnp.sum(p.reshape(ATTN_ROWS // SUBLANES, SUBLANES, qn), axis=0)
                pj[r:r + ATTN_ROWS, 0:qn] = _bf(p)
            l = jnp.sum(l8, axis=0, keepdims=True)
            ot = _dot(vt_ref[j * V_DIM:(j + 1) * V_DIM, 0:kl], pj[0:kl, 0:qn])
            ot_ref[j * V_DIM:(j + 1) * V_DIM, qs:qs + qn] = ot / l
    o_ref[0] = _bf(jnp.transpose(ot_ref[...]))


def _q_blocks(tp, qb):
    blocks = []
    s = 0
    while s < tp:
        n = min(qb, tp - s)
        blocks.append((s, n))
        s += n
    return tuple(blocks)


def _stage_attn(q, k, v, B, Tp):
    H = MLA_HEADS
    q3 = q.reshape(B, Tp, H * HEAD_PAD)
    k3 = k.reshape(B, Tp, H * HEAD_PAD)
    v3 = v.reshape(B, Tp, H * V_DIM)
    o = pl.pallas_call(
        functools.partial(_attn_kernel, q_blocks=_q_blocks(Tp, ATTN_QBLOCK)),
        grid=(B, H // 2),
        in_specs=[pl.BlockSpec((1, Tp, 2 * HEAD_PAD), lambda b, hp: (b, 0, hp)),
                  pl.BlockSpec((1, Tp, 2 * HEAD_PAD), lambda b, hp: (b, 0, hp)),
                  pl.BlockSpec((1, Tp, 2 * V_DIM), lambda b, hp: (b, 0, hp))],
        out_specs=pl.BlockSpec((1, Tp, 2 * V_DIM), lambda b, hp: (b, 0, hp)),
        out_shape=jax.ShapeDtypeStruct((B, Tp, H * V_DIM), jnp.bfloat16),
        scratch_shapes=[pltpu.VMEM((2, Tp, ATTN_QBLOCK), jnp.float32),
                        pltpu.VMEM((2, Tp, ATTN_QBLOCK), jnp.bfloat16),
                        pltpu.VMEM((2 * V_DIM, Tp), jnp.bfloat16), pltpu.VMEM((2 * V_DIM, Tp), jnp.float32)],
        compiler_params=pltpu.CompilerParams(dimension_semantics=("parallel", "parallel"),
                                             vmem_limit_bytes=VMEM_LIMIT),
        name="attn",
    )(q3, k3, v3)
    return o.reshape(B * Tp, H * V_DIM)


def _take_topk(entry, nkeys, emit):
    def step(k, carry):
        vals = [entry(key)[...] for key in range(nkeys)]
        idxs = [float(key) for key in range(nkeys)]
        while len(vals) > 1:
            nv, ni = [], []
            for j in range(0, len(vals) - 1, 2):
                right = vals[j + 1] > vals[j]
                nv.append(jnp.where(right, vals[j + 1], vals[j]))
                ni.append(jnp.where(right, idxs[j + 1], idxs[j]))
            if len(vals) % 2:
                nv.append(vals[-1])
                ni.append(idxs[-1])
            vals, idxs = nv, ni
        m, first = vals[0], idxs[0]
        emit(k, m, first)
        mark = _TAKEN * (1.0 + jnp.asarray(k, jnp.float32) * (1.0 / 32.0))
        for key in range(nkeys):
            ref = entry(key)
            ref[...] = jnp.where(first == float(key), mark, ref[...])
        return carry

    lax.fori_loop(0, PEER_TOPK, step, 0)


def _merge_kernel(h_ref, yr_ref, o_ref, gatt_ref, wao_ref, wout_ref, g2_ref, wq_ref, k1_ref, k2_ref,
                  h1_ref, n2t_ref, c_ref, q1_ref, r2_ref, e2_ref,
                  s_ref, e_ref, r_ref, t_ref, f_ref, cd_ref, t2_ref, f2_ref):
    TM = h_ref.shape[0]
    LG = TM // LANES
    NK = PEER_NKEYS
    H = PEER_HEADS
    f32 = jnp.float32

    y_att = _dot(o_ref[...], wao_ref[...])
    mixed = yr_ref[...].astype(f32) + jax.nn.sigmoid(gatt_ref[...].astype(f32)) * y_att
    h1 = h_ref[...] + _dot(_bf(mixed), wout_ref[...])
    h1_ref[...] = h1
    n2 = _bf(_rms(h1, g2_ref[...]))
    n2t_ref[...] = _bf(jnp.transpose(n2.astype(f32)))
    qp = _bf(_dot(n2, wq_ref[...]))

    for h in range(H):
        for side, kref in enumerate((k1_ref, k2_ref)):
            qs = qp[:, (2 * h + side) * NK:(2 * h + side + 1) * NK]
            st = _dot_nt(kref[...], qs)
            for lg in range(LG):
                s_ref[lg, pl.ds(side * NKH + h, NK, stride=H), :] = st[:, lg * LANES:(lg + 1) * LANES]

    for side in range(2):
        rows = slice(side * NKH, (side + 1) * NKH)
        s0 = s_ref[:, rows, :].reshape(LG, NK, H, LANES)
        e_ref[:, rows, :] = jnp.exp(s0 - jnp.max(s0, axis=1, keepdims=True)).reshape(LG, NKH, LANES)

    def slab(ref, side, key):
        return ref.at[:, side * NKH + key * H:side * NKH + (key + 1) * H, :]

    for side in range(2):
        def emit(k, m, first, side=side):
            t_ref[k, side] = m
            f_ref[k, side] = first
        _take_topk(lambda key, side=side: slab(s_ref, side, key), NK, emit)

    for p, (k1, k2) in enumerate(_CAND):
        cd_ref[p] = t_ref[k1, 0] + t_ref[k2, 1]

    def emit2(k, m, first):
        t2_ref[k] = m
        f2_ref[k] = first
    _take_topk(lambda p: cd_ref.at[p], _NCAND, emit2)

    top = t2_ref[0]
    z = jnp.exp(t2_ref[0] - top)
    for k in range(1, PEER_TOPK):
        z = z + jnp.exp(t2_ref[k] - top)
    inv_z = 1.0 / z
    at_least = []
    for k1 in range(PEER_TOPK):
        n = jnp.zeros((LG, H, LANES), f32)
        for k in range(PEER_TOPK):
            n = n + jnp.where(f2_ref[k] >= float(_ROW_START[k1]), 1.0, 0.0)
        at_least.append(n)
    at_least.append(jnp.zeros((LG, H, LANES), f32))
    counts = [at_least[k1] - at_least[k1 + 1] for k1 in range(PEER_TOPK)]

    for key in range(NK):
        cslab = jnp.zeros((LG, H, LANES), f32)
        for k1 in range(PEER_TOPK):
            cslab = jnp.where(f_ref[k1, 0] == float(key), counts[k1], cslab)
        qslab = slab(e_ref, 0, key)[...] * inv_z
        s2 = slab(s_ref, 1, key)[...]
        rank2 = jnp.where(s2 <= _TAKEN, (s2 * (1.0 / _TAKEN) - 1.0) * 32.0, float(PEER_TOPK))
        r_ref[:, key * H:(key + 1) * H, :] = rank2
        for lg in range(LG):
            cols = slice(lg * LANES, (lg + 1) * LANES)
            c_ref[key * H:(key + 1) * H, cols] = cslab[lg]
            q1_ref[key * H:(key + 1) * H, cols] = qslab[lg]
    for lg in range(LG):
        cols = slice(lg * LANES, (lg + 1) * LANES)
        for h in range(H):
            r2_ref[h * NK:(h + 1) * NK, cols] = _bf(r_ref[lg, pl.ds(h, NK, stride=H), :])
            e2_ref[h * NK:(h + 1) * NK, cols] = _bf(e_ref[lg, pl.ds(NKH + h, NK, stride=H), :])


def _stage_merge(h0, yrg, o, gatt, w_attn_out, w_out, norm2_g, peer_wq, peer_keys1, peer_keys2):
    N, D = h0.shape
    f32 = jnp.float32
    TM = 256
    LG = TM // LANES
    tok = lambda width: pl.BlockSpec((TM, width), lambda i: (i, 0))
    colb = lambda rows: pl.BlockSpec((rows, TM), lambda i: (0, i))
    return pl.pallas_call(
        _merge_kernel,
        grid=(N // TM,),
        in_specs=[tok(D), tok(D), tok(D), tok(D), _const_spec((D, D)), _const_spec((D, D)),
                  _const_spec((1, D)), _const_spec(peer_wq.shape),
                  _const_spec(peer_keys1.shape), _const_spec(peer_keys2.shape)],
        out_specs=[tok(D), colb(D), colb(NKH), colb(NKH), colb(NKH), colb(NKH)],
        out_shape=[jax.ShapeDtypeStruct((N, D), f32), jax.ShapeDtypeStruct((D, N), jnp.bfloat16)]
                  + [jax.ShapeDtypeStruct((NKH, N), f32)] * 2
                  + [jax.ShapeDtypeStruct((NKH, N), jnp.bfloat16)] * 2,
        scratch_shapes=[pltpu.VMEM((LG, 2 * NKH, LANES), f32), pltpu.VMEM((LG, 2 * NKH, LANES), f32),
                        pltpu.VMEM((LG, NKH, LANES), f32),
                        pltpu.VMEM((PEER_TOPK, 2, LG, PEER_HEADS, LANES), f32),
                        pltpu.VMEM((PEER_TOPK, 2, LG, PEER_HEADS, LANES), f32),
                        pltpu.VMEM((_NCAND, LG, PEER_HEADS, LANES), f32),
                        pltpu.VMEM((PEER_TOPK, LG, PEER_HEADS, LANES), f32),
                        pltpu.VMEM((PEER_TOPK, LG, PEER_HEADS, LANES), f32)],
        compiler_params=pltpu.CompilerParams(dimension_semantics=("parallel",), vmem_limit_bytes=VMEM_LIMIT),
        name="merge",
    )(h0, yrg, o, gatt, _bf(w_attn_out), _bf(w_out), _row(norm2_g), _bf(peer_wq),
      _bf(peer_keys1), _bf(peer_keys2))


def _gelu_tanh(x):
    c = 0.7978845608028654
    hx = 0.5 * x
    return hx + hx * jnp.tanh(x * (c + (c * 0.044715) * (x * x)))


def _peer_kernel(n2t_ref, c_ref, q1_ref, r2_ref, e2_ref, u_ref, vt_ref, h1_ref, gf_ref,
                 out_ref, acc_ref, act_ref, coef_ref):
    j = pl.program_id(1)
    ET = u_ref.shape[0]
    TT = n2t_ref.shape[1]
    NK = PEER_NKEYS
    H = PEER_HEADS
    groups = ET // NK

    @pl.when(j == 0)
    def _():
        acc_ref[...] = jnp.zeros_like(acc_ref)

    act_ref[...] = _bf(_dot(u_ref[...], n2t_ref[...]))
    BW = PEER_BLOCK_LANES
    packed = (NK // BF16_ROWS, BF16_ROWS, BW)
    zero = jnp.zeros(packed, jnp.bfloat16)
    for g in range(groups):
        crow = c_ref[g * H:(g + 1) * H, :]
        qrow = q1_ref[g * H:(g + 1) * H, :]
        for lb in range(TT // BW):
            cols = slice(lb * BW, (lb + 1) * BW)
            w = None
            for h in range(H):
                r2 = r2_ref[h * NK:(h + 1) * NK, cols].reshape(packed)
                e2 = e2_ref[h * NK:(h + 1) * NK, cols].reshape(packed)
                cb = _bf(jnp.broadcast_to(crow[h:h + 1, cols], (BF16_ROWS, BW)))
                qb = _bf(jnp.broadcast_to(qrow[h:h + 1, cols], (BF16_ROWS, BW)))
                term = jnp.where(r2 < jnp.broadcast_to(cb[None], packed), e2 * qb[None], zero)
                w = term if w is None else w + term
            rows = slice(g * NK, (g + 1) * NK)
            coef_ref[rows, cols] = w.reshape(NK, BW) * _gelu_tanh(act_ref[rows, cols])
    acc_ref[...] += _dot(vt_ref[...], coef_ref[...])

    @pl.when(j == pl.num_programs(1) - 1)
    def _():
        h2 = h1_ref[...] + jnp.transpose(acc_ref[...])
        out_ref[...] = _rms(h2, gf_ref[...])


def _stage_peer(n2t, cc, q1, r2, e2, h1, peer_u, peer_v, final_g):
    N, D = h1.shape
    TT = 1024
    ET = 1024
    NE = peer_u.shape[0]
    u_bf = _bf(peer_u)
    vt_bf = _bf(jnp.transpose(peer_v))
    colt = lambda rows: pl.BlockSpec((rows, TT), lambda i, j: (0, i))
    grp = pl.BlockSpec((ET // PEER_NKEYS * PEER_HEADS, TT), lambda i, j: (j, i))
    return pl.pallas_call(
        _peer_kernel,
        grid=(N // TT, NE // ET),
        in_specs=[colt(D), grp, grp, colt(NKH), colt(NKH),
                  pl.BlockSpec((ET, D), lambda i, j: (j, 0)),
                  pl.BlockSpec((D, ET), lambda i, j: (0, j)),
                  pl.BlockSpec((TT, D), lambda i, j: (i, 0), pipeline_mode=pl.Buffered(1)),
                  pl.BlockSpec((1, D), lambda i, j: (0, 0))],
        out_specs=pl.BlockSpec((TT, D), lambda i, j: (i, 0)),
        out_shape=jax.ShapeDtypeStruct((N, D), jnp.float32),
        scratch_shapes=[pltpu.VMEM((D, TT), jnp.float32), pltpu.VMEM((ET, TT), jnp.bfloat16),
                        pltpu.VMEM((ET, TT), jnp.bfloat16)],
        compiler_params=pltpu.CompilerParams(dimension_semantics=("parallel", "arbitrary"),
                                             vmem_limit_bytes=VMEM_LIMIT),
        name="peer",
    )(n2t, cc, q1, r2, e2, u_bf, vt_bf, h1, _row(final_g))


def kernel(x, meta_tokens, norm1_g, w_in, conv_w, conv_b, rg_wa, rg_ba, rg_wx, rg_bx, rg_lambda,
           w_rnn_out, q_norm_g, w_uq, kv_norm_g, w_ukv, w_attn_out, w_out, norm2_g,
           peer_wq, peer_keys1, peer_keys2, peer_u, peer_v, final_g):
    B, S, D = x.shape
    assert w_in.shape[0] == 1, "single layer"
    T = N_META + S
    Tp = -(-T // BLOCK) * BLOCK
    N = B * Tp

    meta = jnp.broadcast_to(meta_tokens[None].astype(x.dtype), (B, N_META, D))
    h0 = jnp.concatenate([meta, x, jnp.zeros((B, Tp - T, D), x.dtype)], axis=1).reshape(N, D)

    xr, gr, grnn, gatt, q, k, v = _stage_proj(h0, Tp, norm1_g[0], w_in[0], q_norm_g[0], w_uq[0],
                                              kv_norm_g[0], w_ukv[0])
    yrg = _stage_rglru(xr, gr, grnn, B, Tp, conv_w[0], conv_b[0], rg_wa[0], rg_ba[0], rg_wx[0], rg_bx[0],
                       rg_lambda[0], w_rnn_out[0])
    o = _stage_attn(q, k, v, B, Tp)
    h1, n2t, cc, q1, r2, e2 = _stage_merge(h0, yrg, o, gatt, w_attn_out[0], w_out[0], norm2_g[0],
                                           peer_wq[0], peer_keys1[0], peer_keys2[0])
    out = _stage_peer(n2t, cc, q1, r2, e2, h1, peer_u[0], peer_v[0], final_g)
    return out.reshape(B, Tp, D)[:, N_META:N_META + S]
```

```python
import functools

import jax
import jax.numpy as jnp
import numpy as np
from jax import lax
from jax.experimental import pallas as pl
from jax.experimental.pallas import tpu as pltpu

N_META = 16
EPS = 1e-6
BLOCK = 128
RG_BLOCKS = 8
CONV_W = 4
RG_C = 8.0
MLA_HEADS = 16
Q_LORA = 384
KV_LORA = 256
QK_NOPE = 64
QK_ROPE = 32
V_DIM = 64
ROPE_BASE = 10000.0
PEER_HEADS = 8
PEER_NKEYS = 128
PEER_TOPK = 16

LANES = 128
SUBLANES = 8
HEAD_PAD = 128
VMEM_LIMIT = 56 * 1024 * 1024
NKH = PEER_NKEYS * PEER_HEADS

_CAND = [(k1, k2) for k1 in range(PEER_TOPK) for k2 in range(PEER_TOPK)
         if (k1 + 1) * (k2 + 1) <= PEER_TOPK]
_NCAND = len(_CAND)
_ROW_START = [min(p for p, (a, _) in enumerate(_CAND) if a == k1) for k1 in range(PEER_TOPK)] + [_NCAND]

ATTN_QBLOCK = 512
ATTN_ROWS = 32
Q_PRESCALE = float((QK_NOPE + QK_ROPE) ** -0.5 * np.log2(np.e))
BF16_ROWS = 16
PEER_BLOCK_LANES = 256
_TAKEN = -2.0 ** 100


def _rms(x, g):
    return x * lax.rsqrt(jnp.mean(x * x, axis=-1, keepdims=True) + EPS) * g


def _bf(x):
    return x.astype(jnp.bfloat16)


def _dot(a, b):
    return jnp.dot(a, b, preferred_element_type=jnp.float32)


def _dot_nt(a, b):
    return lax.dot_general(a, b, (((1,), (1,)), ((), ())), preferred_element_type=jnp.float32)


def _const_spec(shape):
    nd = len(shape)
    return pl.BlockSpec(shape, lambda *_: (0,) * nd, pipeline_mode=pl.Buffered(1))


def _row(a):
    return a.reshape(1, -1)


def _proj_kernel(h_ref, g1_ref, w1_ref, w2_ref, qg_ref, kvg_ref, wuq_ref, wuk_ref, wuv_ref,
                 ct_ref, st_ref,
                 xr_ref, gr_ref, grnn_ref, gatt_ref, q_ref, k_ref, v_ref):
    D = h_ref.shape[1]
    n1 = _bf(_rms(h_ref[...], g1_ref[...]))
    for idx, o_ref in enumerate((xr_ref, gr_ref, grnn_ref, gatt_ref)):
        o_ref[...] = _bf(_dot(n1, w1_ref[:, idx * D:(idx + 1) * D]))
    lat = _dot(n1, w2_ref[...])
    cq = lat[:, :Q_LORA]
    ckv = lat[:, Q_LORA:Q_LORA + KV_LORA]
    kr = lat[:, Q_LORA + KV_LORA:]
    ct = ct_ref[...]
    st = st_ref[...]

    def rope(blk):
        return blk * ct + pltpu.roll(blk, HEAD_PAD - QK_ROPE, 1) * st

    nq = _bf(_rms(cq, qg_ref[...]))
    nkv = _bf(_rms(ckv, kvg_ref[...]))
    kr_rot = rope(kr)
    for h in range(MLA_HEADS):
        sl = slice(h * HEAD_PAD, (h + 1) * HEAD_PAD)
        q_ref[:, sl] = _bf(rope(_dot(nq, wuq_ref[:, sl])) * Q_PRESCALE)
        k_ref[:, sl] = _bf(_dot(nkv, wuk_ref[:, sl]) + kr_rot)
    v_ref[...] = _bf(_dot(nkv, wuv_ref[...]))


def _half_swap(w):
    half = QK_ROPE // 2
    return jnp.concatenate([-w[..., half:], w[..., :half]], axis=-1)


def _stage_proj(h0, Tp, front, norm1_g, w_in, q_norm_g, w_uq, kv_norm_g, w_ukv):
    N, D = h0.shape
    f32, bf16 = jnp.float32, jnp.bfloat16
    H = MLA_HEADS
    sizes = (D, D, Q_LORA, KV_LORA, QK_ROPE, D, D)
    offs = np.cumsum((0,) + sizes)
    w_xr, w_gr, w_cq, w_ckv, w_kr, w_grnn, w_gatt = [w_in[:, offs[i]:offs[i + 1]] for i in range(7)]
    w1 = _bf(jnp.concatenate([w_xr, w_gr, w_grnn, w_gatt], axis=1))
    w_kr_ext = jnp.concatenate([jnp.zeros((D, QK_NOPE), f32), w_kr, _half_swap(w_kr)], axis=1)
    w2 = _bf(jnp.concatenate([w_cq, w_ckv, w_kr_ext], axis=1))
    wuq = w_uq.reshape(Q_LORA, H, QK_NOPE + QK_ROPE)
    wuq_ext = _bf(jnp.concatenate([wuq, _half_swap(wuq[..., QK_NOPE:])], axis=-1).reshape(Q_LORA, H * HEAD_PAD))
    wukv = w_ukv.reshape(KV_LORA, H, QK_NOPE + V_DIM)
    wuk_ext = _bf(jnp.concatenate([wukv[..., :QK_NOPE], jnp.zeros((KV_LORA, H, HEAD_PAD - QK_NOPE), f32)],
                                  axis=-1).reshape(KV_LORA, H * HEAD_PAD))
    wuv = _bf(wukv[..., QK_NOPE:].reshape(KV_LORA, H * V_DIM))

    half = QK_ROPE // 2
    pos = jnp.arange(Tp, dtype=f32) - float(front)
    inv = jnp.power(ROPE_BASE, -jnp.arange(half, dtype=f32) * (2.0 / QK_ROPE))
    ang = pos[:, None] * inv[None, :]
    cos, sin = jnp.cos(ang), jnp.sin(ang)
    ctab = jnp.concatenate([jnp.ones((Tp, QK_NOPE), f32), cos, cos, jnp.zeros((Tp, QK_ROPE), f32)], axis=1)
    stab = jnp.concatenate([jnp.zeros((Tp, QK_NOPE), f32), sin, sin, jnp.zeros((Tp, QK_ROPE), f32)], axis=1)

    TM = Tp // 4
    npb = Tp // TM
    tok = lambda width: pl.BlockSpec((TM, width), lambda i: (i, 0))
    return pl.pallas_call(
        _proj_kernel,
        grid=(N // TM,),
        in_specs=[tok(D), _const_spec((1, D)), _const_spec(w1.shape), _const_spec(w2.shape),
                  _const_spec((1, Q_LORA)), _const_spec((1, KV_LORA)),
                  _const_spec(wuq_ext.shape), _const_spec(wuk_ext.shape), _const_spec(wuv.shape),
                  pl.BlockSpec((TM, HEAD_PAD), lambda i: (i % npb, 0)),
                  pl.BlockSpec((TM, HEAD_PAD), lambda i: (i % npb, 0))],
        out_specs=[tok(D), tok(D), tok(D), tok(D), tok(H * HEAD_PAD), tok(H * HEAD_PAD), tok(H * V_DIM)],
        out_shape=[jax.ShapeDtypeStruct((N, D), bf16)] * 4
                  + [jax.ShapeDtypeStruct((N, H * HEAD_PAD), bf16)] * 2
                  + [jax.ShapeDtypeStruct((N, H * V_DIM), bf16)],
        compiler_params=pltpu.CompilerParams(dimension_semantics=("parallel",), vmem_limit_bytes=VMEM_LIMIT),
        name="proj",
    )(h0, _row(norm1_g), w1, w2, _row(q_norm_g), _row(kv_norm_g), wuq_ext, wuk_ext, wuv, ctab, stab)


def _rglru_kernel(xr_ref, gr_ref, grnn_ref, cw_ref, cb_ref, wa_ref, ba_ref, wx_ref, bx_ref,
                  lam_ref, wo_ref, out_ref, xe_ref, hs_ref, *, front):
    Tc, D = xr_ref.shape
    c = pl.program_id(1)

    @pl.when(c == 0)
    def _():
        xe_ref[0:SUBLANES, :] = jnp.zeros((SUBLANES, D), jnp.float32)
        hs_ref[...] = jnp.zeros_like(hs_ref)

    x = xr_ref[...].astype(jnp.float32)
    xe_ref[SUBLANES:, :] = x
    y = cb_ref[...] + x * cw_ref[CONV_W - 1:CONV_W, :]
    for kk in range(CONV_W - 1):
        off = SUBLANES - (CONV_W - 1) + kk
        y = y + xe_ref[off:off + Tc, :] * cw_ref[kk:kk + 1, :]
    xe_ref[0:SUBLANES, :] = xe_ref[Tc:Tc + SUBLANES, :]

    yb = _bf(y)
    bs = D // RG_BLOCKS
    r = jnp.concatenate([_dot(yb[:, n * bs:(n + 1) * bs], wa_ref[n]) for n in range(RG_BLOCKS)], axis=1)
    i = jnp.concatenate([_dot(yb[:, n * bs:(n + 1) * bs], wx_ref[n]) for n in range(RG_BLOCKS)], axis=1)
    r = jax.nn.sigmoid(r + ba_ref[...])
    i = jax.nn.sigmoid(i + bx_ref[...])
    lam = lam_ref[...]
    log_sig = jnp.minimum(lam, 0.0) - jnp.log(1.0 + jnp.exp(-jnp.abs(lam)))
    log_a = RG_C * r * log_sig
    a = jnp.exp(log_a)
    u = jnp.sqrt(1.0 - jnp.exp(2.0 * log_a)) * (i * y)

    row = lax.broadcasted_iota(jnp.int32, (Tc, D), 0)
    u = jnp.where(row + c * Tc >= front, u, 0.0)
    d = 1
    while d < Tc:
        keep = row >= d
        a_prev = jnp.where(keep, pltpu.roll(a, d, 0), 1.0)
        u_prev = jnp.where(keep, pltpu.roll(u, d, 0), 0.0)
        u = a * u_prev + u
        a = a * a_prev
        d *= 2
    h = u + a * hs_ref[0:1, :]
    hs_ref[...] = jnp.broadcast_to(h[Tc - 1:Tc, :], hs_ref.shape)

    gate = jax.nn.gelu(gr_ref[...].astype(jnp.float32))
    y_rnn = _dot(_bf(h * gate), wo_ref[...])
    out_ref[...] = _bf(jax.nn.sigmoid(grnn_ref[...].astype(jnp.float32)) * y_rnn)


def _stage_rglru(xr, gr, grnn, B, Tp, front, conv_w, conv_b, rg_wa, rg_ba, rg_wx, rg_bx, rg_lambda, w_rnn_out):
    N, D = xr.shape
    Tc = Tp // 4
    ncb = Tp // Tc
    seq = lambda: pl.BlockSpec((Tc, D), lambda b, c: (b * ncb + c, 0))
    return pl.pallas_call(
        functools.partial(_rglru_kernel, front=front),
        grid=(B, ncb),
        in_specs=[seq(), seq(), seq(), _const_spec((CONV_W, D)), _const_spec((1, D)),
                  _const_spec(rg_wa.shape), _const_spec((1, D)),
                  _const_spec(rg_wx.shape), _const_spec((1, D)), _const_spec((1, D)),
                  _const_spec((D, D))],
        out_specs=seq(),
        out_shape=jax.ShapeDtypeStruct((N, D), jnp.bfloat16),
        scratch_shapes=[pltpu.VMEM((Tc + SUBLANES, D), jnp.float32), pltpu.VMEM((SUBLANES, D), jnp.float32)],
        compiler_params=pltpu.CompilerParams(dimension_semantics=("parallel", "arbitrary"),
                                             vmem_limit_bytes=VMEM_LIMIT),
        name="rglru",
    )(xr, gr, grnn, conv_w, _row(conv_b), _bf(rg_wa), _row(rg_ba), _bf(rg_wx), _row(rg_bx),
      _row(rg_lambda), _bf(w_rnn_out))


def _attn_kernel(q_ref, k_ref, v_ref, o_ref, s_ref, p_ref, vt_ref, ot_ref, *, q_blocks, front):
    f32 = jnp.float32
    vt_ref[...] = _bf(jnp.transpose(v_ref[0].astype(f32)))
    for (qs, qn) in q_blocks:
        kl = qs + qn
        for j in range(2):
            sj, pj = s_ref.at[j], p_ref.at[j]
            qh = q_ref[0, qs:qs + qn, j * HEAD_PAD:(j + 1) * HEAD_PAD]
            kh = k_ref[0, 0:kl, j * HEAD_PAD:(j + 1) * HEAD_PAD]
            sj[0:kl, 0:qn] = _dot_nt(kh, qh)
            key = lax.broadcasted_iota(jnp.int32, (qn, qn), 0)
            qry = lax.broadcasted_iota(jnp.int32, (qn, qn), 1)
            sj[qs:kl, 0:qn] = jnp.where(key <= qry, sj[qs:kl, 0:qn], -1e30)
            pad_rows = -(-front // SUBLANES) * SUBLANES
            is_key = lax.broadcasted_iota(jnp.int32, (pad_rows, qn), 0) >= front
            sj[0:pad_rows, 0:qn] = jnp.where(is_key, sj[0:pad_rows, 0:qn], -1e30)
            m8 = jnp.full((SUBLANES, qn), -jnp.inf, f32)
            for r in range(0, kl, ATTN_ROWS):
                blk = sj[r:r + ATTN_ROWS, 0:qn].reshape(ATTN_ROWS // SUBLANES, SUBLANES, qn)
                m8 = jnp.maximum(m8, jnp.max(blk, axis=0))
            m = jnp.max(m8, axis=0, keepdims=True)
            l8 = jnp.zeros((SUBLANES, qn), f32)
            for r in range(0, kl, ATTN_ROWS):
                p = jnp.exp2(sj[r:r + ATTN_ROWS, 0:qn] - m)
                l8 = l8 + jnp.sum(p.reshape(ATTN_ROWS // SUBLANES, SUBLANES, qn), axis=0)
                pj[r:r + ATTN_ROWS, 0:qn] = _bf(p)
            l = jnp.sum(l8, axis=0, keepdims=True)
            ot = _dot(vt_ref[j * V_DIM:(j + 1) * V_DIM, 0:kl], pj[0:kl, 0:qn])
            ot_ref[j * V_DIM:(j + 1) * V_DIM, qs:qs + qn] = ot / l
    o_ref[0] = _bf(jnp.transpose(ot_ref[...]))


def _q_blocks(tp, qb):
    blocks = []
    s = 0
    while s < tp:
        n = min(qb, tp - s)
        blocks.append((s, n))
        s += n
    return tuple(blocks)


def _stage_attn(q, k, v, B, Tp, front):
    H = MLA_HEADS
    q3 = q.reshape(B, Tp, H * HEAD_PAD)
    k3 = k.reshape(B, Tp, H * HEAD_PAD)
    v3 = v.reshape(B, Tp, H * V_DIM)
    o = pl.pallas_call(
        functools.partial(_attn_kernel, q_blocks=_q_blocks(Tp, ATTN_QBLOCK), front=front),
        grid=(B, H // 2),
        in_specs=[pl.BlockSpec((1, Tp, 2 * HEAD_PAD), lambda b, hp: (b, 0, hp)),
                  pl.BlockSpec((1, Tp, 2 * HEAD_PAD), lambda b, hp: (b, 0, hp)),
                  pl.BlockSpec((1, Tp, 2 * V_DIM), lambda b, hp: (b, 0, hp))],
        out_specs=pl.BlockSpec((1, Tp, 2 * V_DIM), lambda b, hp: (b, 0, hp)),
        out_shape=jax.ShapeDtypeStruct((B, Tp, H * V_DIM), jnp.bfloat16),
        scratch_shapes=[pltpu.VMEM((2, Tp, ATTN_QBLOCK), jnp.float32),
                        pltpu.VMEM((2, Tp, ATTN_QBLOCK), jnp.bfloat16),
                        pltpu.VMEM((2 * V_DIM, Tp), jnp.bfloat16), pltpu.VMEM((2 * V_DIM, Tp), jnp.float32)],
        compiler_params=pltpu.CompilerParams(dimension_semantics=("parallel", "parallel"),
                                             vmem_limit_bytes=VMEM_LIMIT),
        name="attn",
    )(q3, k3, v3)
    return o.reshape(B * Tp, H * V_DIM)


def _take_topk(entry, nkeys, emit):
    def step(k, carry):
        vals = [entry(key)[...] for key in range(nkeys)]
        idxs = [float(key) for key in range(nkeys)]
        while len(vals) > 1:
            nv, ni = [], []
            for j in range(0, len(vals) - 1, 2):
                right = vals[j + 1] > vals[j]
                nv.append(jnp.where(right, vals[j + 1], vals[j]))
                ni.append(jnp.where(right, idxs[j + 1], idxs[j]))
            if len(vals) % 2:
                nv.append(vals[-1])
                ni.append(idxs[-1])
            vals, idxs = nv, ni
        m, first = vals[0], idxs[0]
        emit(k, m, first)
        mark = _TAKEN * (1.0 + jnp.asarray(k, jnp.float32) * (1.0 / 32.0))
        for key in range(nkeys):
            ref = entry(key)
            ref[...] = jnp.where(first == float(key), mark, ref[...])
        return carry

    lax.fori_loop(0, PEER_TOPK, step, 0)


def _merge_kernel(ha_ref, hb_ref, yra_ref, yrb_ref, oa_ref, ob_ref, ga_ref, gb_ref,
                  wao_ref, wout_ref, g2_ref, wq_ref, k1_ref, k2_ref,
                  h1_ref, n2t_ref, c_ref, q1_ref, r2_ref, e2_ref,
                  s_ref, e_ref, r_ref, t_ref, f_ref, cd_ref, t2_ref, f2_ref):
    TM = h1_ref.shape[0]
    LG = TM // LANES
    NK = PEER_NKEYS
    H = PEER_HEADS
    f32 = jnp.float32

    def rows(a_ref, b_ref):
        return jnp.concatenate([a_ref[...], b_ref[...]], axis=0)

    y_att = _dot(rows(oa_ref, ob_ref), wao_ref[...])
    mixed = rows(yra_ref, yrb_ref).astype(f32) + jax.nn.sigmoid(rows(ga_ref, gb_ref).astype(f32)) * y_att
    h1 = rows(ha_ref, hb_ref) + _dot(_bf(mixed), wout_ref[...])
    h1_ref[...] = h1
    n2 = _bf(_rms(h1, g2_ref[...]))
    n2t_ref[...] = _bf(jnp.transpose(n2.astype(f32)))
    qp = _bf(_dot(n2, wq_ref[...]))

    for h in range(H):
        for side, kref in enumerate((k1_ref, k2_ref)):
            qs = qp[:, (2 * h + side) * NK:(2 * h + side + 1) * NK]
            st = _dot_nt(kref[...], qs)
            for lg in range(LG):
                s_ref[lg, pl.ds(side * NKH + h, NK, stride=H), :] = st[:, lg * LANES:(lg + 1) * LANES]

    for side in range(2):
        rows = slice(side * NKH, (side + 1) * NKH)
        s0 = s_ref[:, rows, :].reshape(LG, NK, H, LANES)
        e_ref[:, rows, :] = jnp.exp(s0 - jnp.max(s0, axis=1, keepdims=True)).reshape(LG, NKH, LANES)

    def slab(ref, side, key):
        return ref.at[:, side * NKH + key * H:side * NKH + (key + 1) * H, :]

    for side in range(2):
        def emit(k, m, first, side=side):
            t_ref[k, side] = m
            f_ref[k, side] = first
        _take_topk(lambda key, side=side: slab(s_ref, side, key), NK, emit)

    for p, (k1, k2) in enumerate(_CAND):
        cd_ref[p] = t_ref[k1, 0] + t_ref[k2, 1]

    def emit2(k, m, first):
        t2_ref[k] = m
        f2_ref[k] = first
    _take_topk(lambda p: cd_ref.at[p], _NCAND, emit2)

    top = t2_ref[0]
    z = jnp.exp(t2_ref[0] - top)
    for k in range(1, PEER_TOPK):
        z = z + jnp.exp(t2_ref[k] - top)
    inv_z = 1.0 / z
    at_least = []
    for k1 in range(PEER_TOPK):
        n = jnp.zeros((LG, H, LANES), f32)
        for k in range(PEER_TOPK):
            n = n + jnp.where(f2_ref[k] >= float(_ROW_START[k1]), 1.0, 0.0)
        at_least.append(n)
    at_least.append(jnp.zeros((LG, H, LANES), f32))
    counts = [at_least[k1] - at_least[k1 + 1] for k1 in range(PEER_TOPK)]

    for key in range(NK):
        cslab = jnp.zeros((LG, H, LANES), f32)
        for k1 in range(PEER_TOPK):
            cslab = jnp.where(f_ref[k1, 0] == float(key), counts[k1], cslab)
        qslab = slab(e_ref, 0, key)[...] * inv_z
        s2 = slab(s_ref, 1, key)[...]
        rank2 = jnp.where(s2 <= _TAKEN, (s2 * (1.0 / _TAKEN) - 1.0) * 32.0, float(PEER_TOPK))
        r_ref[:, key * H:(key + 1) * H, :] = rank2
        for lg in range(LG):
            cols = slice(lg * LANES, (lg + 1) * LANES)
            c_ref[key * H:(key + 1) * H, cols] = cslab[lg]
            q1_ref[key * H:(key + 1) * H, cols] = qslab[lg]
    for lg in range(LG):
        cols = slice(lg * LANES, (lg + 1) * LANES)
        for h in range(H):
            r2_ref[h * NK:(h + 1) * NK, cols] = _bf(r_ref[lg, pl.ds(h, NK, stride=H), :])
            e2_ref[h * NK:(h + 1) * NK, cols] = _bf(e_ref[lg, pl.ds(NKH + h, NK, stride=H), :])


def _stage_merge(h0, yrg, o, gatt, B, Tp, S, w_attn_out, w_out, norm2_g, peer_wq, peer_keys1, peer_keys2):
    D = h0.shape[1]
    N = B * S
    f32 = jnp.float32
    TM = 256
    LG = TM // LANES
    HB = TM // 2
    assert S % TM == 0 and (Tp - S) % HB == 0
    tiles = S // TM
    first = (Tp - S) // HB

    def half(which):
        return pl.BlockSpec((None, HB, D), lambda i: (i // tiles, first + 2 * (i % tiles) + which, 0))

    tok_in = [half(0), half(1)] * 4
    as3 = lambda a: a.reshape(B, Tp, D)
    tok = lambda width: pl.BlockSpec((TM, width), lambda i: (i, 0))
    colb = lambda rows: pl.BlockSpec((rows, TM), lambda i: (0, i))
    return pl.pallas_call(
        _merge_kernel,
        grid=(N // TM,),
        in_specs=tok_in + [_const_spec((D, D)), _const_spec((D, D)),
                           _const_spec((1, D)), _const_spec(peer_wq.shape),
                           _const_spec(peer_keys1.shape), _const_spec(peer_keys2.shape)],
        out_specs=[tok(D), colb(D), colb(NKH), colb(NKH), colb(NKH), colb(NKH)],
        out_shape=[jax.ShapeDtypeStruct((N, D), f32), jax.ShapeDtypeStruct((D, N), jnp.bfloat16)]
                  + [jax.ShapeDtypeStruct((NKH, N), f32)] * 2
                  + [jax.ShapeDtypeStruct((NKH, N), jnp.bfloat16)] * 2,
        scratch_shapes=[pltpu.VMEM((LG, 2 * NKH, LANES), f32), pltpu.VMEM((LG, 2 * NKH, LANES), f32),
                        pltpu.VMEM((LG, NKH, LANES), f32),
                        pltpu.VMEM((PEER_TOPK, 2, LG, PEER_HEADS, LANES), f32),
                        pltpu.VMEM((PEER_TOPK, 2, LG, PEER_HEADS, LANES), f32),
                        pltpu.VMEM((_NCAND, LG, PEER_HEADS, LANES), f32),
                        pltpu.VMEM((PEER_TOPK, LG, PEER_HEADS, LANES), f32),
                        pltpu.VMEM((PEER_TOPK, LG, PEER_HEADS, LANES), f32)],
        compiler_params=pltpu.CompilerParams(dimension_semantics=("parallel",), vmem_limit_bytes=VMEM_LIMIT),
        name="merge",
    )(as3(h0), as3(h0), as3(yrg), as3(yrg), as3(o), as3(o), as3(gatt), as3(gatt),
      _bf(w_attn_out), _bf(w_out), _row(norm2_g), _bf(peer_wq), _bf(peer_keys1), _bf(peer_keys2))


def _gelu_tanh(x):
    c = 0.7978845608028654
    hx = 0.5 * x
    return hx + hx * jnp.tanh(x * (c + (c * 0.044715) * (x * x)))


def _peer_kernel(n2t_ref, c_ref, q1_ref, r2_ref, e2_ref, u_ref, vt_ref, h1_ref, gf_ref,
                 out_ref, acc_ref, act_ref, coef_ref):
    j = pl.program_id(1)
    ET = u_ref.shape[0]
    TT = n2t_ref.shape[1]
    NK = PEER_NKEYS
    H = PEER_HEADS
    groups = ET // NK

    @pl.when(j == 0)
    def _():
        acc_ref[...] = jnp.zeros_like(acc_ref)

    act_ref[...] = _bf(_dot(u_ref[...], n2t_ref[...]))
    BW = PEER_BLOCK_LANES
    packed = (NK // BF16_ROWS, BF16_ROWS, BW)
    zero = jnp.zeros(packed, jnp.bfloat16)
    for g in range(groups):
        crow = c_ref[g * H:(g + 1) * H, :]
        qrow = q1_ref[g * H:(g + 1) * H, :]
        for lb in range(TT // BW):
            cols = slice(lb * BW, (lb + 1) * BW)
            w = None
            for h in range(H):
                r2 = r2_ref[h * NK:(h + 1) * NK, cols].reshape(packed)
                e2 = e2_ref[h * NK:(h + 1) * NK, cols].reshape(packed)
                cb = _bf(jnp.broadcast_to(crow[h:h + 1, cols], (BF16_ROWS, BW)))
                qb = _bf(jnp.broadcast_to(qrow[h:h + 1, cols], (BF16_ROWS, BW)))
                term = jnp.where(r2 < jnp.broadcast_to(cb[None], packed), e2 * qb[None], zero)
                w = term if w is None else w + term
            rows = slice(g * NK, (g + 1) * NK)
            coef_ref[rows, cols] = w.reshape(NK, BW) * _gelu_tanh(act_ref[rows, cols])
    acc_ref[...] += _dot(vt_ref[...], coef_ref[...])

    @pl.when(j == pl.num_programs(1) - 1)
    def _():
        h2 = h1_ref[...] + jnp.transpose(acc_ref[...])
        out_ref[...] = _rms(h2, gf_ref[...])


def _stage_peer(n2t, cc, q1, r2, e2, h1, peer_u, peer_v, final_g):
    N, D = h1.shape
    TT = 1024
    ET = 1024
    NE = peer_u.shape[0]
    u_bf = _bf(peer_u)
    vt_bf = _bf(jnp.transpose(peer_v))
    colt = lambda rows: pl.BlockSpec((rows, TT), lambda i, j: (0, i))
    grp = pl.BlockSpec((ET // PEER_NKEYS * PEER_HEADS, TT), lambda i, j: (j, i))
    return pl.pallas_call(
        _peer_kernel,
        grid=(N // TT, NE // ET),
        in_specs=[colt(D), grp, grp, colt(NKH), colt(NKH),
                  pl.BlockSpec((ET, D), lambda i, j: (j, 0)),
                  pl.BlockSpec((D, ET), lambda i, j: (0, j)),
                  pl.BlockSpec((TT, D), lambda i, j: (i, 0), pipeline_mode=pl.Buffered(1)),
                  pl.BlockSpec((1, D), lambda i, j: (0, 0))],
        out_specs=pl.BlockSpec((TT, D), lambda i, j: (i, 0)),
        out_shape=jax.ShapeDtypeStruct((N, D), jnp.float32),
        scratch_shapes=[pltpu.VMEM((D, TT), jnp.float32), pltpu.VMEM((ET, TT), jnp.bfloat16),
                        pltpu.VMEM((ET, TT), jnp.bfloat16)],
        compiler_params=pltpu.CompilerParams(dimension_semantics=("parallel", "arbitrary"),
                                             vmem_limit_bytes=VMEM_LIMIT),
        name="peer",
    )(n2t, cc, q1, r2, e2, u_bf, vt_bf, h1, _row(final_g))


def kernel(x, meta_tokens, norm1_g, w_in, conv_w, conv_b, rg_wa, rg_ba, rg_wx, rg_bx, rg_lambda,
           w_rnn_out, q_norm_g, w_uq, kv_norm_g, w_ukv, w_attn_out, w_out, norm2_g,
           peer_wq, peer_keys1, peer_keys2, peer_u, peer_v, final_g):
    B, S, D = x.shape
    assert w_in.shape[0] == 1, "single layer"
    T = N_META + S
    Tp = -(-T // BLOCK) * BLOCK
    N = B * Tp

    front = Tp - T
    meta = jnp.broadcast_to(meta_tokens[None].astype(x.dtype), (B, N_META, D))
    h0 = jnp.concatenate([jnp.zeros((B, front, D), x.dtype), meta, x], axis=1).reshape(N, D)

    xr, gr, grnn, gatt, q, k, v = _stage_proj(h0, Tp, front, norm1_g[0], w_in[0], q_norm_g[0], w_uq[0],
                                              kv_norm_g[0], w_ukv[0])
    yrg = _stage_rglru(xr, gr, grnn, B, Tp, front, conv_w[0], conv_b[0], rg_wa[0], rg_ba[0], rg_wx[0],
                       rg_bx[0], rg_lambda[0], w_rnn_out[0])
    o = _stage_attn(q, k, v, B, Tp, front)
    h1, n2t, cc, q1, r2, e2 = _stage_merge(h0, yrg, o, gatt, B, Tp, S, w_attn_out[0], w_out[0], norm2_g[0],
                                           peer_wq[0], peer_keys1[0], peer_keys2[0])
    out = _stage_peer(n2t, cc, q1, r2, e2, h1, peer_u[0], peer_v[0], final_g)
    return out.reshape(B, S, D)
```

```python
import functools

import jax
import jax.numpy as jnp
import numpy as np
from jax import lax
from jax.experimental import pallas as pl
from jax.experimental.pallas import tpu as pltpu

N_META = 16
EPS = 1e-6
BLOCK = 128
RG_BLOCKS = 8
CONV_W = 4
RG_C = 8.0
MLA_HEADS = 16
Q_LORA = 384
KV_LORA = 256
QK_NOPE = 64
QK_ROPE = 32
V_DIM = 64
ROPE_BASE = 10000.0
PEER_HEADS = 8
PEER_NKEYS = 128
PEER_TOPK = 16

LANES = 128
SUBLANES = 8
HEAD_PAD = 128
VMEM_LIMIT = 56 * 1024 * 1024
NKH = PEER_NKEYS * PEER_HEADS

_CAND = [(k1, k2) for k1 in range(PEER_TOPK) for k2 in range(PEER_TOPK)
         if (k1 + 1) * (k2 + 1) <= PEER_TOPK]
_NCAND = len(_CAND)
_ROW_START = [min(p for p, (a, _) in enumerate(_CAND) if a == k1) for k1 in range(PEER_TOPK)] + [_NCAND]

ATTN_QBLOCK = 512
ATTN_ROWS = 32
Q_PRESCALE = float((QK_NOPE + QK_ROPE) ** -0.5 * np.log2(np.e))
BF16_ROWS = 16
PEER_BLOCK_LANES = 256
_TAKEN = -2.0 ** 100


def _rms(x, g):
    return x * lax.rsqrt(jnp.mean(x * x, axis=-1, keepdims=True) + EPS) * g


def _bf(x):
    return x.astype(jnp.bfloat16)


def _dot(a, b):
    return jnp.dot(a, b, preferred_element_type=jnp.float32)


def _dot_nt(a, b):
    return lax.dot_general(a, b, (((1,), (1,)), ((), ())), preferred_element_type=jnp.float32)


def _const_spec(shape):
    nd = len(shape)
    return pl.BlockSpec(shape, lambda *_: (0,) * nd, pipeline_mode=pl.Buffered(1))


def _row(a):
    return a.reshape(1, -1)


def _proj_kernel(h_ref, g1_ref, w1_ref, w2_ref, qg_ref, kvg_ref, wuq_ref, wuk_ref, wuv_ref,
                 ct_ref, st_ref,
                 xr_ref, gr_ref, grnn_ref, gatt_ref, q_ref, k_ref, v_ref):
    D = h_ref.shape[1]
    n1 = _bf(_rms(h_ref[...], g1_ref[...]))
    for idx, o_ref in enumerate((xr_ref, gr_ref, grnn_ref, gatt_ref)):
        o_ref[...] = _bf(_dot(n1, w1_ref[:, idx * D:(idx + 1) * D]))
    lat = _dot(n1, w2_ref[...])
    cq = lat[:, :Q_LORA]
    ckv = lat[:, Q_LORA:Q_LORA + KV_LORA]
    kr = lat[:, Q_LORA + KV_LORA:]
    ct = ct_ref[...]
    st = st_ref[...]

    def rope(blk):
        return blk * ct + pltpu.roll(blk, HEAD_PAD - QK_ROPE, 1) * st

    nq = _bf(_rms(cq, qg_ref[...]))
    nkv = _bf(_rms(ckv, kvg_ref[...]))
    kr_rot = rope(kr)
    for h in range(MLA_HEADS):
        sl = slice(h * HEAD_PAD, (h + 1) * HEAD_PAD)
        q_ref[:, sl] = _bf(rope(_dot(nq, wuq_ref[:, sl])) * Q_PRESCALE)
        k_ref[:, sl] = _bf(_dot(nkv, wuk_ref[:, sl]) + kr_rot)
    v_ref[...] = _bf(_dot(nkv, wuv_ref[...]))


def _half_swap(w):
    half = QK_ROPE // 2
    return jnp.concatenate([-w[..., half:], w[..., :half]], axis=-1)


def _stage_proj(h0, Tp, front, norm1_g, w_in, q_norm_g, w_uq, kv_norm_g, w_ukv):
    N, D = h0.shape
    f32, bf16 = jnp.float32, jnp.bfloat16
    H = MLA_HEADS
    sizes = (D, D, Q_LORA, KV_LORA, QK_ROPE, D, D)
    offs = np.cumsum((0,) + sizes)
    w_xr, w_gr, w_cq, w_ckv, w_kr, w_grnn, w_gatt = [w_in[:, offs[i]:offs[i + 1]] for i in range(7)]
    w1 = _bf(jnp.concatenate([w_xr, w_gr, w_grnn, w_gatt], axis=1))
    w_kr_ext = jnp.concatenate([jnp.zeros((D, QK_NOPE), f32), w_kr, _half_swap(w_kr)], axis=1)
    w2 = _bf(jnp.concatenate([w_cq, w_ckv, w_kr_ext], axis=1))
    wuq = w_uq.reshape(Q_LORA, H, QK_NOPE + QK_ROPE)
    wuq_ext = _bf(jnp.concatenate([wuq, _half_swap(wuq[..., QK_NOPE:])], axis=-1).reshape(Q_LORA, H * HEAD_PAD))
    wukv = w_ukv.reshape(KV_LORA, H, QK_NOPE + V_DIM)
    wuk_ext = _bf(jnp.concatenate([wukv[..., :QK_NOPE], jnp.zeros((KV_LORA, H, HEAD_PAD - QK_NOPE), f32)],
                                  axis=-1).reshape(KV_LORA, H * HEAD_PAD))
    wuv = _bf(wukv[..., QK_NOPE:].reshape(KV_LORA, H * V_DIM))

    half = QK_ROPE // 2
    pos = jnp.arange(Tp, dtype=f32) - float(front)
    inv = jnp.power(ROPE_BASE, -jnp.arange(half, dtype=f32) * (2.0 / QK_ROPE))
    ang = pos[:, None] * inv[None, :]
    cos, sin = jnp.cos(ang), jnp.sin(ang)
    ctab = jnp.concatenate([jnp.ones((Tp, QK_NOPE), f32), cos, cos, jnp.zeros((Tp, QK_ROPE), f32)], axis=1)
    stab = jnp.concatenate([jnp.zeros((Tp, QK_NOPE), f32), sin, sin, jnp.zeros((Tp, QK_ROPE), f32)], axis=1)

    TM = Tp // 4
    npb = Tp // TM
    tok = lambda width: pl.BlockSpec((TM, width), lambda i: (i, 0))
    return pl.pallas_call(
        _proj_kernel,
        grid=(N // TM,),
        in_specs=[tok(D), _const_spec((1, D)), _const_spec(w1.shape), _const_spec(w2.shape),
                  _const_spec((1, Q_LORA)), _const_spec((1, KV_LORA)),
                  _const_spec(wuq_ext.shape), _const_spec(wuk_ext.shape), _const_spec(wuv.shape),
                  pl.BlockSpec((TM, HEAD_PAD), lambda i: (i % npb, 0)),
                  pl.BlockSpec((TM, HEAD_PAD), lambda i: (i % npb, 0))],
        out_specs=[tok(D), tok(D), tok(D), tok(D), tok(H * HEAD_PAD), tok(H * HEAD_PAD), tok(H * V_DIM)],
        out_shape=[jax.ShapeDtypeStruct((N, D), bf16)] * 4
                  + [jax.ShapeDtypeStruct((N, H * HEAD_PAD), bf16)] * 2
                  + [jax.ShapeDtypeStruct((N, H * V_DIM), bf16)],
        compiler_params=pltpu.CompilerParams(dimension_semantics=("parallel",), vmem_limit_bytes=VMEM_LIMIT),
        name="proj",
    )(h0, _row(norm1_g), w1, w2, _row(q_norm_g), _row(kv_norm_g), wuq_ext, wuk_ext, wuv, ctab, stab)


def _rglru_kernel(xr_ref, gr_ref, grnn_ref, cw_ref, cb_ref, wa_ref, ba_ref, wx_ref, bx_ref,
                  lam_ref, wo_ref, out_ref, xe_ref, hs_ref, hseq_ref, *, front):
    Tc, D = xr_ref.shape
    c = pl.program_id(1)

    @pl.when(c == 0)
    def _():
        xe_ref[0:SUBLANES, :] = jnp.zeros((SUBLANES, D), jnp.float32)
        hs_ref[...] = jnp.zeros_like(hs_ref)

    x = xr_ref[...].astype(jnp.float32)
    xe_ref[SUBLANES:, :] = x
    y = cb_ref[...] + x * cw_ref[CONV_W - 1:CONV_W, :]
    for kk in range(CONV_W - 1):
        off = SUBLANES - (CONV_W - 1) + kk
        y = y + xe_ref[off:off + Tc, :] * cw_ref[kk:kk + 1, :]
    xe_ref[0:SUBLANES, :] = xe_ref[Tc:Tc + SUBLANES, :]

    yb = _bf(y)
    bs = D // RG_BLOCKS
    r = jnp.concatenate([_dot(yb[:, n * bs:(n + 1) * bs], wa_ref[n]) for n in range(RG_BLOCKS)], axis=1)
    i = jnp.concatenate([_dot(yb[:, n * bs:(n + 1) * bs], wx_ref[n]) for n in range(RG_BLOCKS)], axis=1)
    r = jax.nn.sigmoid(r + ba_ref[...])
    i = jax.nn.sigmoid(i + bx_ref[...])
    lam = lam_ref[...]
    log_sig = jnp.minimum(lam, 0.0) - jnp.log(1.0 + jnp.exp(-jnp.abs(lam)))
    log_a = RG_C * r * log_sig
    a = jnp.exp(log_a)
    u = jnp.sqrt(1.0 - a * a) * (i * y)

    row = lax.broadcasted_iota(jnp.int32, (Tc, D), 0)
    u = jnp.where(row + c * Tc >= front, u, 0.0)
    groups = Tc // SUBLANES
    a = a.reshape(groups, SUBLANES, D)
    u = u.reshape(groups, SUBLANES, D)
    sub = lax.broadcasted_iota(jnp.int32, (groups, SUBLANES, D), 1)
    d = 1
    while d < SUBLANES:
        keep = sub >= d
        a_prev = jnp.where(keep, pltpu.roll(a, d, 1), 1.0)
        u_prev = jnp.where(keep, pltpu.roll(u, d, 1), 0.0)
        u = a * u_prev + u
        a = a * a_prev
        d *= 2
    carry = hs_ref[...]
    for g in range(groups):
        hg = u[g] + a[g] * carry
        hseq_ref[g * SUBLANES:(g + 1) * SUBLANES, :] = hg
        carry = jnp.broadcast_to(hg[SUBLANES - 1:SUBLANES, :], (SUBLANES, D))
    hs_ref[...] = carry

    gate = jax.nn.gelu(gr_ref[...].astype(jnp.float32))
    y_rnn = _dot(_bf(hseq_ref[...] * gate), wo_ref[...])
    out_ref[...] = _bf(jax.nn.sigmoid(grnn_ref[...].astype(jnp.float32)) * y_rnn)


def _stage_rglru(xr, gr, grnn, B, Tp, front, conv_w, conv_b, rg_wa, rg_ba, rg_wx, rg_bx, rg_lambda, w_rnn_out):
    N, D = xr.shape
    Tc = Tp // 4
    ncb = Tp // Tc
    seq = lambda: pl.BlockSpec((Tc, D), lambda b, c: (b * ncb + c, 0))
    return pl.pallas_call(
        functools.partial(_rglru_kernel, front=front),
        grid=(B, ncb),
        in_specs=[seq(), seq(), seq(), _const_spec((CONV_W, D)), _const_spec((1, D)),
                  _const_spec(rg_wa.shape), _const_spec((1, D)),
                  _const_spec(rg_wx.shape), _const_spec((1, D)), _const_spec((1, D)),
                  _const_spec((D, D))],
        out_specs=seq(),
        out_shape=jax.ShapeDtypeStruct((N, D), jnp.bfloat16),
        scratch_shapes=[pltpu.VMEM((Tc + SUBLANES, D), jnp.float32), pltpu.VMEM((SUBLANES, D), jnp.float32),
                        pltpu.VMEM((Tc, D), jnp.float32)],
        compiler_params=pltpu.CompilerParams(dimension_semantics=("parallel", "arbitrary"),
                                             vmem_limit_bytes=VMEM_LIMIT),
        name="rglru",
    )(xr, gr, grnn, conv_w, _row(conv_b), _bf(rg_wa), _row(rg_ba), _bf(rg_wx), _row(rg_bx),
      _row(rg_lambda), _bf(w_rnn_out))


def _attn_kernel(q_ref, k_ref, v_ref, o_ref, s_ref, p_ref, vt_ref, ot_ref, *, q_blocks, front):
    f32 = jnp.float32
    vt_ref[...] = _bf(jnp.transpose(v_ref[0].astype(f32)))
    for (qs, qn) in q_blocks:
        kl = qs + qn
        for j in range(2):
            sj, pj = s_ref.at[j], p_ref.at[j]
            qh = q_ref[0, qs:qs + qn, j * HEAD_PAD:(j + 1) * HEAD_PAD]
            kh = k_ref[0, 0:kl, j * HEAD_PAD:(j + 1) * HEAD_PAD]
            sj[0:kl, 0:qn] = _dot_nt(kh, qh)
            key = lax.broadcasted_iota(jnp.int32, (qn, qn), 0)
            qry = lax.broadcasted_iota(jnp.int32, (qn, qn), 1)
            sj[qs:kl, 0:qn] = jnp.where(key <= qry, sj[qs:kl, 0:qn], -1e30)
            pad_rows = -(-front // SUBLANES) * SUBLANES
            is_key = lax.broadcasted_iota(jnp.int32, (pad_rows, qn), 0) >= front
            sj[0:pad_rows, 0:qn] = jnp.where(is_key, sj[0:pad_rows, 0:qn], -1e30)
            m8 = jnp.full((SUBLANES, qn), -jnp.inf, f32)
            for r in range(0, kl, ATTN_ROWS):
                blk = sj[r:r + ATTN_ROWS, 0:qn].reshape(ATTN_ROWS // SUBLANES, SUBLANES, qn)
                m8 = jnp.maximum(m8, jnp.max(blk, axis=0))
            m = jnp.max(m8, axis=0, keepdims=True)
            l8 = jnp.zeros((SUBLANES, qn), f32)
            for r in range(0, kl, ATTN_ROWS):
                p = jnp.exp2(sj[r:r + ATTN_ROWS, 0:qn] - m)
                l8 = l8 + jnp.sum(p.reshape(ATTN_ROWS // SUBLANES, SUBLANES, qn), axis=0)
                pj[r:r + ATTN_ROWS, 0:qn] = _bf(p)
            l = jnp.sum(l8, axis=0, keepdims=True)
            ot = _dot(vt_ref[j * V_DIM:(j + 1) * V_DIM, 0:kl], pj[0:kl, 0:qn])
            ot_ref[j * V_DIM:(j + 1) * V_DIM, qs:qs + qn] = ot / l
    o_ref[0] = _bf(jnp.transpose(ot_ref[...]))


def _q_blocks(tp, qb):
    blocks = []
    s = 0
    while s < tp:
        n = min(qb, tp - s)
        blocks.append((s, n))
        s += n
    return tuple(blocks)


def _stage_attn(q, k, v, B, Tp, front):
    H = MLA_HEADS
    q3 = q.reshape(B, Tp, H * HEAD_PAD)
    k3 = k.reshape(B, Tp, H * HEAD_PAD)
    v3 = v.reshape(B, Tp, H * V_DIM)
    o = pl.pallas_call(
        functools.partial(_attn_kernel, q_blocks=_q_blocks(Tp, ATTN_QBLOCK), front=front),
        grid=(B, H // 2),
        in_specs=[pl.BlockSpec((1, Tp, 2 * HEAD_PAD), lambda b, hp: (b, 0, hp)),
                  pl.BlockSpec((1, Tp, 2 * HEAD_PAD), lambda b, hp: (b, 0, hp)),
                  pl.BlockSpec((1, Tp, 2 * V_DIM), lambda b, hp: (b, 0, hp))],
        out_specs=pl.BlockSpec((1, Tp, 2 * V_DIM), lambda b, hp: (b, 0, hp)),
        out_shape=jax.ShapeDtypeStruct((B, Tp, H * V_DIM), jnp.bfloat16),
        scratch_shapes=[pltpu.VMEM((2, Tp, ATTN_QBLOCK), jnp.float32),
                        pltpu.VMEM((2, Tp, ATTN_QBLOCK), jnp.bfloat16),
                        pltpu.VMEM((2 * V_DIM, Tp), jnp.bfloat16), pltpu.VMEM((2 * V_DIM, Tp), jnp.float32)],
        compiler_params=pltpu.CompilerParams(dimension_semantics=("parallel", "parallel"),
                                             vmem_limit_bytes=VMEM_LIMIT),
        name="attn",
    )(q3, k3, v3)
    return o.reshape(B * Tp, H * V_DIM)


def _take_topk(entry, nkeys, emit):
    def step(k, carry):
        vals = [entry(key)[...] for key in range(nkeys)]
        idxs = [float(key) for key in range(nkeys)]
        while len(vals) > 1:
            nv, ni = [], []
            for j in range(0, len(vals) - 1, 2):
                right = vals[j + 1] > vals[j]
                nv.append(jnp.where(right, vals[j + 1], vals[j]))
                ni.append(jnp.where(right, idxs[j + 1], idxs[j]))
            if len(vals) % 2:
                nv.append(vals[-1])
                ni.append(idxs[-1])
            vals, idxs = nv, ni
        m, first = vals[0], idxs[0]
        emit(k, m, first)
        mark = _TAKEN * (1.0 + jnp.asarray(k, jnp.float32) * (1.0 / 32.0))
        for key in range(nkeys):
            ref = entry(key)
            ref[...] = jnp.where(first == float(key), mark, ref[...])
        return carry

    lax.fori_loop(0, PEER_TOPK, step, 0)


def _merge_kernel(ha_ref, hb_ref, yra_ref, yrb_ref, oa_ref, ob_ref, ga_ref, gb_ref,
                  wao_ref, wout_ref, g2_ref, wq_ref, k1_ref, k2_ref,
                  h1_ref, n2t_ref, c_ref, q1_ref, r2_ref, e2_ref,
                  s_ref, e_ref, r_ref, t_ref, f_ref, cd_ref, t2_ref, f2_ref):
    TM = h1_ref.shape[0]
    LG = TM // LANES
    NK = PEER_NKEYS
    H = PEER_HEADS
    f32 = jnp.float32

    def rows(a_ref, b_ref):
        return jnp.concatenate([a_ref[...], b_ref[...]], axis=0)

    y_att = _dot(rows(oa_ref, ob_ref), wao_ref[...])
    mixed = rows(yra_ref, yrb_ref).astype(f32) + jax.nn.sigmoid(rows(ga_ref, gb_ref).astype(f32)) * y_att
    h1 = rows(ha_ref, hb_ref) + _dot(_bf(mixed), wout_ref[...])
    h1_ref[...] = h1
    n2 = _bf(_rms(h1, g2_ref[...]))
    n2t_ref[...] = _bf(jnp.transpose(n2.astype(f32)))
    qp = _bf(_dot(n2, wq_ref[...]))

    for h in range(H):
        for side, kref in enumerate((k1_ref, k2_ref)):
            qs = qp[:, (2 * h + side) * NK:(2 * h + side + 1) * NK]
            st = _dot_nt(kref[...], qs)
            for lg in range(LG):
                s_ref[lg, pl.ds(side * NKH + h, NK, stride=H), :] = st[:, lg * LANES:(lg + 1) * LANES]

    for side in range(2):
        rows = slice(side * NKH, (side + 1) * NKH)
        s0 = s_ref[:, rows, :].reshape(LG, NK, H, LANES)
        e_ref[:, rows, :] = jnp.exp(s0 - jnp.max(s0, axis=1, keepdims=True)).reshape(LG, NKH, LANES)

    def slab(ref, side, key):
        return ref.at[:, side * NKH + key * H:side * NKH + (key + 1) * H, :]

    for side in range(2):
        def emit(k, m, first, side=side):
            t_ref[k, side] = m
            f_ref[k, side] = first
        _take_topk(lambda key, side=side: slab(s_ref, side, key), NK, emit)

    for p, (k1, k2) in enumerate(_CAND):
        cd_ref[p] = t_ref[k1, 0] + t_ref[k2, 1]

    def emit2(k, m, first):
        t2_ref[k] = m
        f2_ref[k] = first
    _take_topk(lambda p: cd_ref.at[p], _NCAND, emit2)

    top = t2_ref[0]
    z = jnp.exp(t2_ref[0] - top)
    for k in range(1, PEER_TOPK):
        z = z + jnp.exp(t2_ref[k] - top)
    inv_z = 1.0 / z
    at_least = []
    for k1 in range(PEER_TOPK):
        n = jnp.zeros((LG, H, LANES), f32)
        for k in range(PEER_TOPK):
            n = n + jnp.where(f2_ref[k] >= float(_ROW_START[k1]), 1.0, 0.0)
        at_least.append(n)
    at_least.append(jnp.zeros((LG, H, LANES), f32))
    counts = [at_least[k1] - at_least[k1 + 1] for k1 in range(PEER_TOPK)]

    for key in range(NK):
        cslab = jnp.zeros((LG, H, LANES), f32)
        for k1 in range(PEER_TOPK):
            cslab = jnp.where(f_ref[k1, 0] == float(key), counts[k1], cslab)
        qslab = slab(e_ref, 0, key)[...] * inv_z
        s2 = slab(s_ref, 1, key)[...]
        rank2 = jnp.where(s2 <= _TAKEN, (s2 * (1.0 / _TAKEN) - 1.0) * 32.0, float(PEER_TOPK))
        r_ref[:, key * H:(key + 1) * H, :] = rank2
        for lg in range(LG):
            cols = slice(lg * LANES, (lg + 1) * LANES)
            c_ref[key * H:(key + 1) * H, cols] = cslab[lg]
            q1_ref[key * H:(key + 1) * H, cols] = qslab[lg]
    for lg in range(LG):
        cols = slice(lg * LANES, (lg + 1) * LANES)
        for h in range(H):
            r2_ref[h * NK:(h + 1) * NK, cols] = _bf(r_ref[lg, pl.ds(h, NK, stride=H), :])
            e2_ref[h * NK:(h + 1) * NK, cols] = _bf(e_ref[lg, pl.ds(NKH + h, NK, stride=H), :])


def _stage_merge(h0, yrg, o, gatt, B, Tp, S, w_attn_out, w_out, norm2_g, peer_wq, peer_keys1, peer_keys2):
    D = h0.shape[1]
    N = B * S
    f32 = jnp.float32
    TM = 256
    LG = TM // LANES
    HB = TM // 2
    assert S % TM == 0 and (Tp - S) % HB == 0
    tiles = S // TM
    first = (Tp - S) // HB

    def half(which):
        return pl.BlockSpec((None, HB, D), lambda i: (i // tiles, first + 2 * (i % tiles) + which, 0))

    tok_in = [half(0), half(1)] * 4
    as3 = lambda a: a.reshape(B, Tp, D)
    tok = lambda width: pl.BlockSpec((TM, width), lambda i: (i, 0))
    colb = lambda rows: pl.BlockSpec((rows, TM), lambda i: (0, i))
    return pl.pallas_call(
        _merge_kernel,
        grid=(N // TM,),
        in_specs=tok_in + [_const_spec((D, D)), _const_spec((D, D)),
                           _const_spec((1, D)), _const_spec(peer_wq.shape),
                           _const_spec(peer_keys1.shape), _const_spec(peer_keys2.shape)],
        out_specs=[tok(D), colb(D), colb(NKH), colb(NKH), colb(NKH), colb(NKH)],
        out_shape=[jax.ShapeDtypeStruct((N, D), f32), jax.ShapeDtypeStruct((D, N), jnp.bfloat16)]
                  + [jax.ShapeDtypeStruct((NKH, N), f32)] * 2
                  + [jax.ShapeDtypeStruct((NKH, N), jnp.bfloat16)] * 2,
        scratch_shapes=[pltpu.VMEM((LG, 2 * NKH, LANES), f32), pltpu.VMEM((LG, 2 * NKH, LANES), f32),
                        pltpu.VMEM((LG, NKH, LANES), f32),
                        pltpu.VMEM((PEER_TOPK, 2, LG, PEER_HEADS, LANES), f32),
                        pltpu.VMEM((PEER_TOPK, 2, LG, PEER_HEADS, LANES), f32),
                        pltpu.VMEM((_NCAND, LG, PEER_HEADS, LANES), f32),
                        pltpu.VMEM((PEER_TOPK, LG, PEER_HEADS, LANES), f32),
                        pltpu.VMEM((PEER_TOPK, LG, PEER_HEADS, LANES), f32)],
        compiler_params=pltpu.CompilerParams(dimension_semantics=("parallel",), vmem_limit_bytes=VMEM_LIMIT),
        name="merge",
    )(as3(h0), as3(h0), as3(yrg), as3(yrg), as3(o), as3(o), as3(gatt), as3(gatt),
      _bf(w_attn_out), _bf(w_out), _row(norm2_g), _bf(peer_wq), _bf(peer_keys1), _bf(peer_keys2))


def _gelu_tanh(x):
    c = 0.7978845608028654
    hx = 0.5 * x
    return hx + hx * jnp.tanh(x * (c + (c * 0.044715) * (x * x)))


def _peer_kernel(n2t_ref, c_ref, q1_ref, r2_ref, e2_ref, u_ref, vt_ref, h1_ref, gf_ref,
                 out_ref, acc_ref, act_ref, coef_ref):
    j = pl.program_id(1)
    ET = u_ref.shape[0]
    TT = n2t_ref.shape[1]
    NK = PEER_NKEYS
    H = PEER_HEADS
    groups = ET // NK

    @pl.when(j == 0)
    def _():
        acc_ref[...] = jnp.zeros_like(acc_ref)

    act_ref[...] = _bf(_dot(u_ref[...], n2t_ref[...]))
    BW = PEER_BLOCK_LANES
    packed = (NK // BF16_ROWS, BF16_ROWS, BW)
    zero = jnp.zeros(packed, jnp.bfloat16)
    for g in range(groups):
        crow = c_ref[g * H:(g + 1) * H, :]
        qrow = q1_ref[g * H:(g + 1) * H, :]
        for lb in range(TT // BW):
            cols = slice(lb * BW, (lb + 1) * BW)
            w = None
            for h in range(H):
                r2 = r2_ref[h * NK:(h + 1) * NK, cols].reshape(packed)
                e2 = e2_ref[h * NK:(h + 1) * NK, cols].reshape(packed)
                cb = _bf(jnp.broadcast_to(crow[h:h + 1, cols], (BF16_ROWS, BW)))
                qb = _bf(jnp.broadcast_to(qrow[h:h + 1, cols], (BF16_ROWS, BW)))
                term = jnp.where(r2 < jnp.broadcast_to(cb[None], packed), e2 * qb[None], zero)
                w = term if w is None else w + term
            rows = slice(g * NK, (g + 1) * NK)
            coef_ref[rows, cols] = w.reshape(NK, BW) * _gelu_tanh(act_ref[rows, cols])
    acc_ref[...] += _dot(vt_ref[...], coef_ref[...])

    @pl.when(j == pl.num_programs(1) - 1)
    def _():
        h2 = h1_ref[...] + jnp.transpose(acc_ref[...])
        out_ref[...] = _rms(h2, gf_ref[...])


def _stage_peer(n2t, cc, q1, r2, e2, h1, peer_u, peer_v, final_g):
    N, D = h1.shape
    TT = 1024
    ET = 1024
    NE = peer_u.shape[0]
    u_bf = _bf(peer_u)
    vt_bf = _bf(jnp.transpose(peer_v))
    colt = lambda rows: pl.BlockSpec((rows, TT), lambda i, j: (0, i))
    grp = pl.BlockSpec((ET // PEER_NKEYS * PEER_HEADS, TT), lambda i, j: (j, i))
    return pl.pallas_call(
        _peer_kernel,
        grid=(N // TT, NE // ET),
        in_specs=[colt(D), grp, grp, colt(NKH), colt(NKH),
                  pl.BlockSpec((ET, D), lambda i, j: (j, 0)),
                  pl.BlockSpec((D, ET), lambda i, j: (0, j)),
                  pl.BlockSpec((TT, D), lambda i, j: (i, 0), pipeline_mode=pl.Buffered(1)),
                  pl.BlockSpec((1, D), lambda i, j: (0, 0))],
        out_specs=pl.BlockSpec((TT, D), lambda i, j: (i, 0)),
        out_shape=jax.ShapeDtypeStruct((N, D), jnp.float32),
        scratch_shapes=[pltpu.VMEM((D, TT), jnp.float32), pltpu.VMEM((ET, TT), jnp.bfloat16),
                        pltpu.VMEM((ET, TT), jnp.bfloat16)],
        compiler_params=pltpu.CompilerParams(dimension_semantics=("parallel", "arbitrary"),
                                             vmem_limit_bytes=VMEM_LIMIT),
        name="peer",
    )(n2t, cc, q1, r2, e2, u_bf, vt_bf, h1, _row(final_g))


def kernel(x, meta_tokens, norm1_g, w_in, conv_w, conv_b, rg_wa, rg_ba, rg_wx, rg_bx, rg_lambda,
           w_rnn_out, q_norm_g, w_uq, kv_norm_g, w_ukv, w_attn_out, w_out, norm2_g,
           peer_wq, peer_keys1, peer_keys2, peer_u, peer_v, final_g):
    B, S, D = x.shape
    assert w_in.shape[0] == 1, "single layer"
    T = N_META + S
    Tp = -(-T // BLOCK) * BLOCK
    N = B * Tp

    front = Tp - T
    meta = jnp.broadcast_to(meta_tokens[None].astype(x.dtype), (B, N_META, D))
    h0 = jnp.concatenate([jnp.zeros((B, front, D), x.dtype), meta, x], axis=1).reshape(N, D)

    xr, gr, grnn, gatt, q, k, v = _stage_proj(h0, Tp, front, norm1_g[0], w_in[0], q_norm_g[0], w_uq[0],
                                              kv_norm_g[0], w_ukv[0])
    yrg = _stage_rglru(xr, gr, grnn, B, Tp, front, conv_w[0], conv_b[0], rg_wa[0], rg_ba[0], rg_wx[0],
                       rg_bx[0], rg_lambda[0], w_rnn_out[0])
    o = _stage_attn(q, k, v, B, Tp, front)
    h1, n2t, cc, q1, r2, e2 = _stage_merge(h0, yrg, o, gatt, B, Tp, S, w_attn_out[0], w_out[0], norm2_g[0],
                                           peer_wq[0], peer_keys1[0], peer_keys2[0])
    out = _stage_peer(n2t, cc, q1, r2, e2, h1, peer_u[0], peer_v[0], final_g)
    return out.reshape(B, S, D)
```

```python
import functools

import jax
import jax.numpy as jnp
import numpy as np
from jax import lax
from jax.experimental import pallas as pl
from jax.experimental.pallas import tpu as pltpu

N_META = 16
EPS = 1e-6
BLOCK = 128
RG_BLOCKS = 8
CONV_W = 4
RG_C = 8.0
MLA_HEADS = 16
Q_LORA = 384
KV_LORA = 256
QK_NOPE = 64
QK_ROPE = 32
V_DIM = 64
ROPE_BASE = 10000.0
PEER_HEADS = 8
PEER_NKEYS = 128
PEER_TOPK = 16

LANES = 128
SUBLANES = 8
HEAD_PAD = 128
VMEM_LIMIT = 56 * 1024 * 1024
NKH = PEER_NKEYS * PEER_HEADS

_CAND = [(k1, k2) for k1 in range(PEER_TOPK) for k2 in range(PEER_TOPK)
         if (k1 + 1) * (k2 + 1) <= PEER_TOPK]
_NCAND = len(_CAND)
_ROW_START = [min(p for p, (a, _) in enumerate(_CAND) if a == k1) for k1 in range(PEER_TOPK)] + [_NCAND]

ATTN_QBLOCK = 512
ATTN_ROWS = 32
Q_PRESCALE = float((QK_NOPE + QK_ROPE) ** -0.5 * np.log2(np.e))
BF16_ROWS = 16
PEER_BLOCK_LANES = 256
SEQ_CHUNKS = 4
MERGE_TOKENS = 256
PEER_TOKENS = 1024
PEER_EXPERT_TILE = 1024
_TAKEN = -2.0 ** 100


def _rms(x, g):
    return x * lax.rsqrt(jnp.mean(x * x, axis=-1, keepdims=True) + EPS) * g


def _bf(x):
    return x.astype(jnp.bfloat16)


def _dot(a, b):
    return jnp.dot(a, b, preferred_element_type=jnp.float32)


def _dot_nt(a, b):
    return lax.dot_general(a, b, (((1,), (1,)), ((), ())), preferred_element_type=jnp.float32)


def _const_spec(shape):
    nd = len(shape)
    return pl.BlockSpec(shape, lambda *_: (0,) * nd, pipeline_mode=pl.Buffered(1))


def _row(a):
    return a.reshape(1, -1)


def _proj_kernel(h_ref, g1_ref, w1_ref, w2_ref, qg_ref, kvg_ref, wuq_ref, wuk_ref, wuv_ref,
                 ct_ref, st_ref,
                 xr_ref, gr_ref, grnn_ref, gatt_ref, q_ref, k_ref, v_ref):
    D = h_ref.shape[1]
    n1 = _bf(_rms(h_ref[...], g1_ref[...]))
    for idx, o_ref in enumerate((xr_ref, gr_ref, grnn_ref, gatt_ref)):
        o_ref[...] = _bf(_dot(n1, w1_ref[:, idx * D:(idx + 1) * D]))
    lat = _dot(n1, w2_ref[...])
    cq = lat[:, :Q_LORA]
    ckv = lat[:, Q_LORA:Q_LORA + KV_LORA]
    kr = lat[:, Q_LORA + KV_LORA:]
    ct = ct_ref[...]
    st = st_ref[...]

    def rope(blk):
        return blk * ct + pltpu.roll(blk, HEAD_PAD - QK_ROPE, 1) * st

    nq = _bf(_rms(cq, qg_ref[...]))
    nkv = _bf(_rms(ckv, kvg_ref[...]))
    kr_rot = rope(kr)
    for h in range(MLA_HEADS):
        sl = slice(h * HEAD_PAD, (h + 1) * HEAD_PAD)
        q_ref[:, sl] = _bf(rope(_dot(nq, wuq_ref[:, sl])) * Q_PRESCALE)
        k_ref[:, sl] = _bf(_dot(nkv, wuk_ref[:, sl]) + kr_rot)
    v_ref[...] = _bf(_dot(nkv, wuv_ref[...]))


def _half_swap(w):
    half = QK_ROPE // 2
    return jnp.concatenate([-w[..., half:], w[..., :half]], axis=-1)


def _stage_proj(h0, Tp, front, norm1_g, w_in, q_norm_g, w_uq, kv_norm_g, w_ukv):
    N, D = h0.shape
    f32, bf16 = jnp.float32, jnp.bfloat16
    H = MLA_HEADS
    sizes = (D, D, Q_LORA, KV_LORA, QK_ROPE, D, D)
    offs = np.cumsum((0,) + sizes)
    w_xr, w_gr, w_cq, w_ckv, w_kr, w_grnn, w_gatt = [w_in[:, offs[i]:offs[i + 1]] for i in range(7)]
    w1 = _bf(jnp.concatenate([w_xr, w_gr, w_grnn, w_gatt], axis=1))
    w_kr_ext = jnp.concatenate([jnp.zeros((D, QK_NOPE), f32), w_kr, _half_swap(w_kr)], axis=1)
    w2 = _bf(jnp.concatenate([w_cq, w_ckv, w_kr_ext], axis=1))
    wuq = w_uq.reshape(Q_LORA, H, QK_NOPE + QK_ROPE)
    wuq_ext = _bf(jnp.concatenate([wuq, _half_swap(wuq[..., QK_NOPE:])], axis=-1).reshape(Q_LORA, H * HEAD_PAD))
    wukv = w_ukv.reshape(KV_LORA, H, QK_NOPE + V_DIM)
    wuk_ext = _bf(jnp.concatenate([wukv[..., :QK_NOPE], jnp.zeros((KV_LORA, H, HEAD_PAD - QK_NOPE), f32)],
                                  axis=-1).reshape(KV_LORA, H * HEAD_PAD))
    wuv = _bf(wukv[..., QK_NOPE:].reshape(KV_LORA, H * V_DIM))

    half = QK_ROPE // 2
    pos = jnp.arange(Tp, dtype=f32) - float(front)
    inv = jnp.power(ROPE_BASE, -jnp.arange(half, dtype=f32) * (2.0 / QK_ROPE))
    ang = pos[:, None] * inv[None, :]
    cos, sin = jnp.cos(ang), jnp.sin(ang)
    ctab = jnp.concatenate([jnp.ones((Tp, QK_NOPE), f32), cos, cos, jnp.zeros((Tp, QK_ROPE), f32)], axis=1)
    stab = jnp.concatenate([jnp.zeros((Tp, QK_NOPE), f32), sin, sin, jnp.zeros((Tp, QK_ROPE), f32)], axis=1)

    TM = Tp // SEQ_CHUNKS
    npb = Tp // TM
    tok = lambda width: pl.BlockSpec((TM, width), lambda i: (i, 0))
    return pl.pallas_call(
        _proj_kernel,
        grid=(N // TM,),
        in_specs=[tok(D), _const_spec((1, D)), _const_spec(w1.shape), _const_spec(w2.shape),
                  _const_spec((1, Q_LORA)), _const_spec((1, KV_LORA)),
                  _const_spec(wuq_ext.shape), _const_spec(wuk_ext.shape), _const_spec(wuv.shape),
                  pl.BlockSpec((TM, HEAD_PAD), lambda i: (i % npb, 0)),
                  pl.BlockSpec((TM, HEAD_PAD), lambda i: (i % npb, 0))],
        out_specs=[tok(D), tok(D), tok(D), tok(D), tok(H * HEAD_PAD), tok(H * HEAD_PAD), tok(H * V_DIM)],
        out_shape=[jax.ShapeDtypeStruct((N, D), bf16)] * 4
                  + [jax.ShapeDtypeStruct((N, H * HEAD_PAD), bf16)] * 2
                  + [jax.ShapeDtypeStruct((N, H * V_DIM), bf16)],
        compiler_params=pltpu.CompilerParams(dimension_semantics=("parallel",), vmem_limit_bytes=VMEM_LIMIT),
        name="proj",
    )(h0, _row(norm1_g), w1, w2, _row(q_norm_g), _row(kv_norm_g), wuq_ext, wuk_ext, wuv, ctab, stab)


def _rglru_kernel(xr_ref, gr_ref, grnn_ref, cw_ref, cb_ref, wa_ref, ba_ref, wx_ref, bx_ref,
                  lam_ref, wo_ref, out_ref, xe_ref, hs_ref, hseq_ref, *, front):
    Tc, D = xr_ref.shape
    c = pl.program_id(1)

    @pl.when(c == 0)
    def _():
        xe_ref[0:SUBLANES, :] = jnp.zeros((SUBLANES, D), jnp.float32)
        hs_ref[...] = jnp.zeros_like(hs_ref)

    x = xr_ref[...].astype(jnp.float32)
    xe_ref[SUBLANES:, :] = x
    y = cb_ref[...] + x * cw_ref[CONV_W - 1:CONV_W, :]
    for kk in range(CONV_W - 1):
        off = SUBLANES - (CONV_W - 1) + kk
        y = y + xe_ref[off:off + Tc, :] * cw_ref[kk:kk + 1, :]
    xe_ref[0:SUBLANES, :] = xe_ref[Tc:Tc + SUBLANES, :]

    yb = _bf(y)
    bs = D // RG_BLOCKS
    r = jnp.concatenate([_dot(yb[:, n * bs:(n + 1) * bs], wa_ref[n]) for n in range(RG_BLOCKS)], axis=1)
    i = jnp.concatenate([_dot(yb[:, n * bs:(n + 1) * bs], wx_ref[n]) for n in range(RG_BLOCKS)], axis=1)
    r = jax.nn.sigmoid(r + ba_ref[...])
    i = jax.nn.sigmoid(i + bx_ref[...])
    lam = lam_ref[...]
    log_sig = jnp.minimum(lam, 0.0) - jnp.log(1.0 + jnp.exp(-jnp.abs(lam)))
    log_a = RG_C * r * log_sig
    a = jnp.exp(log_a)
    u = jnp.sqrt(1.0 - a * a) * (i * y)

    row = lax.broadcasted_iota(jnp.int32, (Tc, D), 0)
    u = jnp.where(row + c * Tc >= front, u, 0.0)
    groups = Tc // SUBLANES
    a = a.reshape(groups, SUBLANES, D)
    u = u.reshape(groups, SUBLANES, D)
    sub = lax.broadcasted_iota(jnp.int32, (groups, SUBLANES, D), 1)
    d = 1
    while d < SUBLANES:
        keep = sub >= d
        a_prev = jnp.where(keep, pltpu.roll(a, d, 1), 1.0)
        u_prev = jnp.where(keep, pltpu.roll(u, d, 1), 0.0)
        u = a * u_prev + u
        a = a * a_prev
        d *= 2
    carry = hs_ref[...]
    for g in range(groups):
        hg = u[g] + a[g] * carry
        hseq_ref[g * SUBLANES:(g + 1) * SUBLANES, :] = hg
        carry = jnp.broadcast_to(hg[SUBLANES - 1:SUBLANES, :], (SUBLANES, D))
    hs_ref[...] = carry

    gate = jax.nn.gelu(gr_ref[...].astype(jnp.float32))
    y_rnn = _dot(_bf(hseq_ref[...] * gate), wo_ref[...])
    out_ref[...] = _bf(jax.nn.sigmoid(grnn_ref[...].astype(jnp.float32)) * y_rnn)


def _stage_rglru(xr, gr, grnn, B, Tp, front, conv_w, conv_b, rg_wa, rg_ba, rg_wx, rg_bx, rg_lambda, w_rnn_out):
    N, D = xr.shape
    Tc = Tp // SEQ_CHUNKS
    ncb = Tp // Tc
    seq = lambda: pl.BlockSpec((Tc, D), lambda b, c: (b * ncb + c, 0))
    return pl.pallas_call(
        functools.partial(_rglru_kernel, front=front),
        grid=(B, ncb),
        in_specs=[seq(), seq(), seq(), _const_spec((CONV_W, D)), _const_spec((1, D)),
                  _const_spec(rg_wa.shape), _const_spec((1, D)),
                  _const_spec(rg_wx.shape), _const_spec((1, D)), _const_spec((1, D)),
                  _const_spec((D, D))],
        out_specs=seq(),
        out_shape=jax.ShapeDtypeStruct((N, D), jnp.bfloat16),
        scratch_shapes=[pltpu.VMEM((Tc + SUBLANES, D), jnp.float32), pltpu.VMEM((SUBLANES, D), jnp.float32),
                        pltpu.VMEM((Tc, D), jnp.float32)],
        compiler_params=pltpu.CompilerParams(dimension_semantics=("parallel", "arbitrary"),
                                             vmem_limit_bytes=VMEM_LIMIT),
        name="rglru",
    )(xr, gr, grnn, conv_w, _row(conv_b), _bf(rg_wa), _row(rg_ba), _bf(rg_wx), _row(rg_bx),
      _row(rg_lambda), _bf(w_rnn_out))


def _attn_kernel(q_ref, k_ref, v_ref, o_ref, s_ref, p_ref, vt_ref, ot_ref, *, q_blocks, front):
    f32 = jnp.float32
    vt_ref[...] = _bf(jnp.transpose(v_ref[0].astype(f32)))
    for (qs, qn) in q_blocks:
        kl = qs + qn
        for j in range(2):
            sj, pj = s_ref.at[j], p_ref.at[j]
            qh = q_ref[0, qs:qs + qn, j * HEAD_PAD:(j + 1) * HEAD_PAD]
            kh = k_ref[0, 0:kl, j * HEAD_PAD:(j + 1) * HEAD_PAD]
            sj[0:kl, 0:qn] = _dot_nt(kh, qh)
            key = lax.broadcasted_iota(jnp.int32, (qn, qn), 0)
            qry = lax.broadcasted_iota(jnp.int32, (qn, qn), 1)
            sj[qs:kl, 0:qn] = jnp.where(key <= qry, sj[qs:kl, 0:qn], -1e30)
            pad_rows = -(-front // SUBLANES) * SUBLANES
            is_key = lax.broadcasted_iota(jnp.int32, (pad_rows, qn), 0) >= front
            sj[0:pad_rows, 0:qn] = jnp.where(is_key, sj[0:pad_rows, 0:qn], -1e30)
            m8 = jnp.full((SUBLANES, qn), -jnp.inf, f32)
            for r in range(0, kl, ATTN_ROWS):
                blk = sj[r:r + ATTN_ROWS, 0:qn].reshape(ATTN_ROWS // SUBLANES, SUBLANES, qn)
                m8 = jnp.maximum(m8, jnp.max(blk, axis=0))
            m = jnp.max(m8, axis=0, keepdims=True)
            l8 = jnp.zeros((SUBLANES, qn), f32)
            for r in range(0, kl, ATTN_ROWS):
                p = jnp.exp2(sj[r:r + ATTN_ROWS, 0:qn] - m)
                l8 = l8 + jnp.sum(p.reshape(ATTN_ROWS // SUBLANES, SUBLANES, qn), axis=0)
                pj[r:r + ATTN_ROWS, 0:qn] = _bf(p)
            l = jnp.sum(l8, axis=0, keepdims=True)
            ot = _dot(vt_ref[j * V_DIM:(j + 1) * V_DIM, 0:kl], pj[0:kl, 0:qn])
            ot_ref[j * V_DIM:(j + 1) * V_DIM, qs:qs + qn] = ot / l
    o_ref[0] = _bf(jnp.transpose(ot_ref[...]))


def _q_blocks(tp, qb):
    blocks = []
    s = 0
    while s < tp:
        n = min(qb, tp - s)
        blocks.append((s, n))
        s += n
    return tuple(blocks)


def _stage_attn(q, k, v, B, Tp, front):
    H = MLA_HEADS
    q3 = q.reshape(B, Tp, H * HEAD_PAD)
    k3 = k.reshape(B, Tp, H * HEAD_PAD)
    v3 = v.reshape(B, Tp, H * V_DIM)
    o = pl.pallas_call(
        functools.partial(_attn_kernel, q_blocks=_q_blocks(Tp, ATTN_QBLOCK), front=front),
        grid=(B, H // 2),
        in_specs=[pl.BlockSpec((1, Tp, 2 * HEAD_PAD), lambda b, hp: (b, 0, hp)),
                  pl.BlockSpec((1, Tp, 2 * HEAD_PAD), lambda b, hp: (b, 0, hp)),
                  pl.BlockSpec((1, Tp, 2 * V_DIM), lambda b, hp: (b, 0, hp))],
        out_specs=pl.BlockSpec((1, Tp, 2 * V_DIM), lambda b, hp: (b, 0, hp)),
        out_shape=jax.ShapeDtypeStruct((B, Tp, H * V_DIM), jnp.bfloat16),
        scratch_shapes=[pltpu.VMEM((2, Tp, ATTN_QBLOCK), jnp.float32),
                        pltpu.VMEM((2, Tp, ATTN_QBLOCK), jnp.bfloat16),
                        pltpu.VMEM((2 * V_DIM, Tp), jnp.bfloat16), pltpu.VMEM((2 * V_DIM, Tp), jnp.float32)],
        compiler_params=pltpu.CompilerParams(dimension_semantics=("parallel", "parallel"),
                                             vmem_limit_bytes=VMEM_LIMIT),
        name="attn",
    )(q3, k3, v3)
    return o.reshape(B * Tp, H * V_DIM)


def _mark(k):
    return _TAKEN * (1.0 + jnp.asarray(k, jnp.float32) * (1.0 / 32.0))


def _take_topk(entry, nkeys, emit):
    def step(k, carry):
        vals = [entry(key)[...] for key in range(nkeys)]
        idxs = [float(key) for key in range(nkeys)]
        while len(vals) > 1:
            nv, ni = [], []
            for j in range(0, len(vals) - 1, 2):
                right = vals[j + 1] > vals[j]
                nv.append(jnp.where(right, vals[j + 1], vals[j]))
                ni.append(jnp.where(right, idxs[j + 1], idxs[j]))
            if len(vals) % 2:
                nv.append(vals[-1])
                ni.append(idxs[-1])
            vals, idxs = nv, ni
        m, first = vals[0], idxs[0]
        emit(k, m)
        mark = _mark(k)
        for key in range(nkeys):
            ref = entry(key)
            ref[...] = jnp.where(first == float(key), mark, ref[...])
        return carry

    lax.fori_loop(0, PEER_TOPK, step, 0)


def _take_topk_distinct(entry, nkeys, emit):
    def step(k, carry):
        vals = [entry(key)[...] for key in range(nkeys)]
        while len(vals) > 1:
            nv = [jnp.maximum(vals[j], vals[j + 1]) for j in range(0, len(vals) - 1, 2)]
            if len(vals) % 2:
                nv.append(vals[-1])
            vals = nv
        m = vals[0]
        emit(k, m)
        mark = _mark(k)
        for key in range(nkeys):
            ref = entry(key)
            cur = ref[...]
            ref[...] = jnp.where(cur == m, mark, cur)
        return carry

    lax.fori_loop(0, PEER_TOPK, step, 0)


def _count_taken(entry, nkeys):
    n = None
    for key in range(nkeys):
        one = jnp.where(entry(key)[...] <= _TAKEN, 1.0, 0.0)
        n = one if n is None else n + one
    return n


def _merge_kernel(ha_ref, hb_ref, yra_ref, yrb_ref, oa_ref, ob_ref, ga_ref, gb_ref,
                  wao_ref, wout_ref, g2_ref, wq_ref, k1_ref, k2_ref,
                  h1_ref, n2t_ref, c_ref, q1_ref, r2_ref, e2_ref,
                  s_ref, e_ref, r_ref, t_ref, cd_ref, t2_ref, qp_ref):
    TM = h1_ref.shape[0]
    LG = TM // LANES
    NK = PEER_NKEYS
    H = PEER_HEADS
    f32 = jnp.float32

    def rows(a_ref, b_ref):
        return jnp.concatenate([a_ref[...], b_ref[...]], axis=0)

    y_att = _dot(rows(oa_ref, ob_ref), wao_ref[...])
    mixed = rows(yra_ref, yrb_ref).astype(f32) + jax.nn.sigmoid(rows(ga_ref, gb_ref).astype(f32)) * y_att
    h1 = rows(ha_ref, hb_ref) + _dot(_bf(mixed), wout_ref[...])
    h1_ref[...] = h1
    n2 = _bf(_rms(h1, g2_ref[...]))
    n2t_ref[...] = _bf(jnp.transpose(n2.astype(f32)))
    qp_ref[...] = _bf(_dot(n2, wq_ref[...]))

    def slab(ref, side, key):
        return ref.at[:, side * NKH + key * H:side * NKH + (key + 1) * H, :]

    def select(take):
        for h in range(H):
            for side, kref in enumerate((k1_ref, k2_ref)):
                qs = qp_ref[:, (2 * h + side) * NK:(2 * h + side + 1) * NK]
                st = _dot_nt(kref[...], qs)
                for lg in range(LG):
                    s_ref[lg, pl.ds(side * NKH + h, NK, stride=H), :] = st[:, lg * LANES:(lg + 1) * LANES]
        for side in range(2):
            rows = slice(side * NKH, (side + 1) * NKH)
            s0 = s_ref[:, rows, :].reshape(LG, NK, H, LANES)
            e_ref[:, rows, :] = jnp.exp(s0 - jnp.max(s0, axis=1, keepdims=True)).reshape(LG, NKH, LANES)
        for side in range(2):
            def emit(k, m, side=side):
                t_ref[k, side] = m
            take(lambda key, side=side: slab(s_ref, side, key), NK, emit)
        for p, (k1, k2) in enumerate(_CAND):
            cd_ref[p] = t_ref[k1, 0] + t_ref[k2, 1]

        def emit2(k, m):
            t2_ref[k] = m
        take(lambda p: cd_ref.at[p], _NCAND, emit2)

    select(_take_topk_distinct)
    off = jnp.abs(_count_taken(lambda p: cd_ref.at[p], _NCAND) - float(PEER_TOPK))
    for side in range(2):
        off = off + jnp.abs(_count_taken(lambda key, side=side: slab(s_ref, side, key), NK) - float(PEER_TOPK))

    @pl.when(jnp.max(off) > 0.0)
    def _():
        select(_take_topk)

    top = t2_ref[0]
    z = jnp.exp(t2_ref[0] - top)
    for k in range(1, PEER_TOPK):
        z = z + jnp.exp(t2_ref[k] - top)
    inv_z = 1.0 / z
    counts = [_count_taken(lambda q, k1=k1: cd_ref.at[_ROW_START[k1] + q], _ROW_START[k1 + 1] - _ROW_START[k1])
              for k1 in range(PEER_TOPK)]

    def step_of(s):
        return jnp.where(s <= _TAKEN, (s * (1.0 / _TAKEN) - 1.0) * 32.0, float(PEER_TOPK))

    for key in range(NK):
        rank1 = step_of(slab(s_ref, 0, key)[...])
        cslab = jnp.zeros((LG, H, LANES), f32)
        for k1 in range(PEER_TOPK):
            cslab = jnp.where(rank1 == float(k1), counts[k1], cslab)
        qslab = slab(e_ref, 0, key)[...] * inv_z
        rank2 = step_of(slab(s_ref, 1, key)[...])
        r_ref[:, key * H:(key + 1) * H, :] = rank2
        for lg in range(LG):
            cols = slice(lg * LANES, (lg + 1) * LANES)
            c_ref[key * H:(key + 1) * H, cols] = cslab[lg]
            q1_ref[key * H:(key + 1) * H, cols] = qslab[lg]
    for lg in range(LG):
        cols = slice(lg * LANES, (lg + 1) * LANES)
        for h in range(H):
            r2_ref[h * NK:(h + 1) * NK, cols] = _bf(r_ref[lg, pl.ds(h, NK, stride=H), :])
            e2_ref[h * NK:(h + 1) * NK, cols] = _bf(e_ref[lg, pl.ds(NKH + h, NK, stride=H), :])


def _stage_merge(h0, yrg, o, gatt, B, Tp, S, w_attn_out, w_out, norm2_g, peer_wq, peer_keys1, peer_keys2):
    D = h0.shape[1]
    N = B * S
    f32 = jnp.float32
    TM = MERGE_TOKENS
    LG = TM // LANES
    HB = TM // 2
    assert S % TM == 0 and (Tp - S) % HB == 0
    tiles = S // TM
    first = (Tp - S) // HB

    def half(which):
        return pl.BlockSpec((None, HB, D), lambda i: (i // tiles, first + 2 * (i % tiles) + which, 0))

    tok_in = [half(0), half(1)] * 4
    as3 = lambda a: a.reshape(B, Tp, D)
    tok = lambda width: pl.BlockSpec((TM, width), lambda i: (i, 0))
    colb = lambda rows: pl.BlockSpec((rows, TM), lambda i: (0, i))
    return pl.pallas_call(
        _merge_kernel,
        grid=(N // TM,),
        in_specs=tok_in + [_const_spec((D, D)), _const_spec((D, D)),
                           _const_spec((1, D)), _const_spec(peer_wq.shape),
                           _const_spec(peer_keys1.shape), _const_spec(peer_keys2.shape)],
        out_specs=[tok(D), colb(D), colb(NKH), colb(NKH), colb(NKH), colb(NKH)],
        out_shape=[jax.ShapeDtypeStruct((N, D), f32), jax.ShapeDtypeStruct((D, N), jnp.bfloat16)]
                  + [jax.ShapeDtypeStruct((NKH, N), f32)] * 2
                  + [jax.ShapeDtypeStruct((NKH, N), jnp.bfloat16)] * 2,
        scratch_shapes=[pltpu.VMEM((LG, 2 * NKH, LANES), f32), pltpu.VMEM((LG, 2 * NKH, LANES), f32),
                        pltpu.VMEM((LG, NKH, LANES), f32),
                        pltpu.VMEM((PEER_TOPK, 2, LG, PEER_HEADS, LANES), f32),
                        pltpu.VMEM((_NCAND, LG, PEER_HEADS, LANES), f32),
                        pltpu.VMEM((PEER_TOPK, LG, PEER_HEADS, LANES), f32),
                        pltpu.VMEM((TM, peer_wq.shape[1]), jnp.bfloat16)],
        compiler_params=pltpu.CompilerParams(dimension_semantics=("parallel",), vmem_limit_bytes=VMEM_LIMIT),
        name="merge",
    )(as3(h0), as3(h0), as3(yrg), as3(yrg), as3(o), as3(o), as3(gatt), as3(gatt),
      _bf(w_attn_out), _bf(w_out), _row(norm2_g), _bf(peer_wq), _bf(peer_keys1), _bf(peer_keys2))


def _gelu_tanh(x):
    c = 0.7978845608028654
    hx = 0.5 * x
    return hx + hx * jnp.tanh(x * (c + (c * 0.044715) * (x * x)))


def _peer_kernel(n2t_ref, c_ref, q1_ref, r2_ref, e2_ref, u_ref, vt_ref, h1_ref, gf_ref,
                 out_ref, acc_ref, act_ref, coef_ref):
    j = pl.program_id(1)
    ET = u_ref.shape[0]
    TT = n2t_ref.shape[1]
    NK = PEER_NKEYS
    H = PEER_HEADS
    groups = ET // NK

    @pl.when(j == 0)
    def _():
        acc_ref[...] = jnp.zeros_like(acc_ref)

    act_ref[...] = _bf(_dot(u_ref[...], n2t_ref[...]))
    BW = PEER_BLOCK_LANES
    packed = (NK // BF16_ROWS, BF16_ROWS, BW)
    zero = jnp.zeros(packed, jnp.bfloat16)
    for g in range(groups):
        crow = c_ref[g * H:(g + 1) * H, :]
        qrow = q1_ref[g * H:(g + 1) * H, :]
        for lb in range(TT // BW):
            cols = slice(lb * BW, (lb + 1) * BW)
            w = None
            for h in range(H):
                r2 = r2_ref[h * NK:(h + 1) * NK, cols].reshape(packed)
                e2 = e2_ref[h * NK:(h + 1) * NK, cols].reshape(packed)
                cb = _bf(jnp.broadcast_to(crow[h:h + 1, cols], (BF16_ROWS, BW)))
                qb = _bf(jnp.broadcast_to(qrow[h:h + 1, cols], (BF16_ROWS, BW)))
                term = jnp.where(r2 < jnp.broadcast_to(cb[None], packed), e2 * qb[None], zero)
                w = term if w is None else w + term
            rows = slice(g * NK, (g + 1) * NK)
            coef_ref[rows, cols] = w.reshape(NK, BW) * _gelu_tanh(act_ref[rows, cols])
    acc_ref[...] += _dot(vt_ref[...], coef_ref[...])

    @pl.when(j == pl.num_programs(1) - 1)
    def _():
        h2 = h1_ref[...] + jnp.transpose(acc_ref[...])
        out_ref[...] = _rms(h2, gf_ref[...])


def _stage_peer(n2t, cc, q1, r2, e2, h1, peer_u, peer_v, final_g):
    N, D = h1.shape
    TT = PEER_TOKENS
    ET = PEER_EXPERT_TILE
    assert N % TT == 0 and peer_u.shape[0] % ET == 0
    NE = peer_u.shape[0]
    u_bf = _bf(peer_u)
    vt_bf = _bf(jnp.transpose(peer_v))
    colt = lambda rows: pl.BlockSpec((rows, TT), lambda i, j: (0, i))
    grp = pl.BlockSpec((ET // PEER_NKEYS * PEER_HEADS, TT), lambda i, j: (j, i))
    return pl.pallas_call(
        _peer_kernel,
        grid=(N // TT, NE // ET),
        in_specs=[colt(D), grp, grp, colt(NKH), colt(NKH),
                  pl.BlockSpec((ET, D), lambda i, j: (j, 0)),
                  pl.BlockSpec((D, ET), lambda i, j: (0, j)),
                  pl.BlockSpec((TT, D), lambda i, j: (i, 0), pipeline_mode=pl.Buffered(1)),
                  pl.BlockSpec((1, D), lambda i, j: (0, 0))],
        out_specs=pl.BlockSpec((TT, D), lambda i, j: (i, 0)),
        out_shape=jax.ShapeDtypeStruct((N, D), jnp.float32),
        scratch_shapes=[pltpu.VMEM((D, TT), jnp.float32), pltpu.VMEM((ET, TT), jnp.bfloat16),
                        pltpu.VMEM((ET, TT), jnp.bfloat16)],
        compiler_params=pltpu.CompilerParams(dimension_semantics=("parallel", "arbitrary"),
                                             vmem_limit_bytes=VMEM_LIMIT),
        name="peer",
    )(n2t, cc, q1, r2, e2, u_bf, vt_bf, h1, _row(final_g))


def kernel(x, meta_tokens, norm1_g, w_in, conv_w, conv_b, rg_wa, rg_ba, rg_wx, rg_bx, rg_lambda,
           w_rnn_out, q_norm_g, w_uq, kv_norm_g, w_ukv, w_attn_out, w_out, norm2_g,
           peer_wq, peer_keys1, peer_keys2, peer_u, peer_v, final_g):
    B, S, D = x.shape
    assert w_in.shape[0] == 1, "single layer"
    T = N_META + S
    Tp = -(-T // BLOCK) * BLOCK
    N = B * Tp

    front = Tp - T
    meta = jnp.broadcast_to(meta_tokens[None].astype(x.dtype), (B, N_META, D))
    h0 = jnp.concatenate([jnp.zeros((B, front, D), x.dtype), meta, x], axis=1).reshape(N, D)

    xr, gr, grnn, gatt, q, k, v = _stage_proj(h0, Tp, front, norm1_g[0], w_in[0], q_norm_g[0], w_uq[0],
                                              kv_norm_g[0], w_ukv[0])
    yrg = _stage_rglru(xr, gr, grnn, B, Tp, front, conv_w[0], conv_b[0], rg_wa[0], rg_ba[0], rg_wx[0],
                       rg_bx[0], rg_lambda[0], w_rnn_out[0])
    o = _stage_attn(q, k, v, B, Tp, front)
    h1, n2t, cc, q1, r2, e2 = _stage_merge(h0, yrg, o, gatt, B, Tp, S, w_attn_out[0], w_out[0], norm2_g[0],
                                           peer_wq[0], peer_keys1[0], peer_keys2[0])
    out = _stage_peer(n2t, cc, q1, r2, e2, h1, peer_u[0], peer_v[0], final_g)
    return out.reshape(B, S, D)
```

```python
import functools

import jax
import jax.numpy as jnp
import numpy as np
from jax import lax
from jax.experimental import pallas as pl
from jax.experimental.pallas import tpu as pltpu

N_META = 16
EPS = 1e-6
BLOCK = 128
RG_BLOCKS = 8
CONV_W = 4
RG_C = 8.0
MLA_HEADS = 16
Q_LORA = 384
KV_LORA = 256
QK_NOPE = 64
QK_ROPE = 32
V_DIM = 64
ROPE_BASE = 10000.0
PEER_HEADS = 8
PEER_NKEYS = 128
PEER_TOPK = 16

LANES = 128
SUBLANES = 8
HEAD_PAD = 128
VMEM_LIMIT = 56 * 1024 * 1024
NKH = PEER_NKEYS * PEER_HEADS

_CAND = [(k1, k2) for k1 in range(PEER_TOPK) for k2 in range(PEER_TOPK)
         if (k1 + 1) * (k2 + 1) <= PEER_TOPK]
_NCAND = len(_CAND)
_ROW_START = [min(p for p, (a, _) in enumerate(_CAND) if a == k1) for k1 in range(PEER_TOPK)] + [_NCAND]

ATTN_QBLOCK = 512
ATTN_ROWS = 32
Q_PRESCALE = float((QK_NOPE + QK_ROPE) ** -0.5 * np.log2(np.e))
BF16_ROWS = 16
PEER_BLOCK_LANES = 256
SEQ_CHUNKS = 4
MERGE_TOKENS = 256
PEER_TOKENS = 1024
PEER_EXPERT_TILE = 1024
_TAKEN = -2.0 ** 100


def _rms(x, g):
    return x * lax.rsqrt(jnp.mean(x * x, axis=-1, keepdims=True) + EPS) * g


def _bf(x):
    return x.astype(jnp.bfloat16)


def _dot(a, b):
    return jnp.dot(a, b, preferred_element_type=jnp.float32)


def _dot_nt(a, b):
    return lax.dot_general(a, b, (((1,), (1,)), ((), ())), preferred_element_type=jnp.float32)


def _const_spec(shape):
    nd = len(shape)
    return pl.BlockSpec(shape, lambda *_: (0,) * nd, pipeline_mode=pl.Buffered(1))


def _row(a):
    return a.reshape(1, -1)


def _proj_kernel(h_ref, g1_ref, w1_ref, w2_ref, qg_ref, kvg_ref, wuq_ref, wuk_ref, wuv_ref,
                 ct_ref, st_ref,
                 xr_ref, gr_ref, grnn_ref, gatt_ref, q_ref, k_ref, v_ref):
    D = h_ref.shape[1]
    n1 = _bf(_rms(h_ref[...], g1_ref[...]))
    for idx, o_ref in enumerate((xr_ref, gr_ref, grnn_ref, gatt_ref)):
        o_ref[...] = _bf(_dot(n1, w1_ref[:, idx * D:(idx + 1) * D]))
    lat = _dot(n1, w2_ref[...])
    cq = lat[:, :Q_LORA]
    ckv = lat[:, Q_LORA:Q_LORA + KV_LORA]
    kr = lat[:, Q_LORA + KV_LORA:]
    ct = ct_ref[...]
    st = st_ref[...]

    def rope(blk):
        return blk * ct + pltpu.roll(blk, HEAD_PAD - QK_ROPE, 1) * st

    nq = _bf(_rms(cq, qg_ref[...]))
    nkv = _bf(_rms(ckv, kvg_ref[...]))
    kr_rot = rope(kr)
    for hp in range(MLA_HEADS // 2):
        pair = slice(2 * hp * HEAD_PAD, (2 * hp + 2) * HEAD_PAD)
        q2 = _dot(nq, wuq_ref[:, pair])
        k2 = _dot(nkv, wuk_ref[:, pair])
        for half in range(2):
            sl = slice((2 * hp + half) * HEAD_PAD, (2 * hp + half + 1) * HEAD_PAD)
            cols = slice(half * HEAD_PAD, (half + 1) * HEAD_PAD)
            q_ref[:, sl] = _bf(rope(q2[:, cols]) * Q_PRESCALE)
            k_ref[:, sl] = _bf(k2[:, cols] + kr_rot)
    v_ref[...] = _bf(_dot(nkv, wuv_ref[...]))


def _half_swap(w):
    half = QK_ROPE // 2
    return jnp.concatenate([-w[..., half:], w[..., :half]], axis=-1)


def _stage_proj(h0, Tp, front, norm1_g, w_in, q_norm_g, w_uq, kv_norm_g, w_ukv):
    N, D = h0.shape
    f32, bf16 = jnp.float32, jnp.bfloat16
    H = MLA_HEADS
    sizes = (D, D, Q_LORA, KV_LORA, QK_ROPE, D, D)
    offs = np.cumsum((0,) + sizes)
    w_xr, w_gr, w_cq, w_ckv, w_kr, w_grnn, w_gatt = [w_in[:, offs[i]:offs[i + 1]] for i in range(7)]
    w1 = _bf(jnp.concatenate([w_xr, w_gr, w_grnn, w_gatt], axis=1))
    w_kr_ext = jnp.concatenate([jnp.zeros((D, QK_NOPE), f32), w_kr, _half_swap(w_kr)], axis=1)
    w2 = _bf(jnp.concatenate([w_cq, w_ckv, w_kr_ext], axis=1))
    wuq = w_uq.reshape(Q_LORA, H, QK_NOPE + QK_ROPE)
    wuq_ext = _bf(jnp.concatenate([wuq, _half_swap(wuq[..., QK_NOPE:])], axis=-1).reshape(Q_LORA, H * HEAD_PAD))
    wukv = w_ukv.reshape(KV_LORA, H, QK_NOPE + V_DIM)
    wuk_ext = _bf(jnp.concatenate([wukv[..., :QK_NOPE], jnp.zeros((KV_LORA, H, HEAD_PAD - QK_NOPE), f32)],
                                  axis=-1).reshape(KV_LORA, H * HEAD_PAD))
    wuv = _bf(wukv[..., QK_NOPE:].reshape(KV_LORA, H * V_DIM))

    half = QK_ROPE // 2
    pos = jnp.arange(Tp, dtype=f32) - float(front)
    inv = jnp.power(ROPE_BASE, -jnp.arange(half, dtype=f32) * (2.0 / QK_ROPE))
    ang = pos[:, None] * inv[None, :]
    cos, sin = jnp.cos(ang), jnp.sin(ang)
    ctab = jnp.concatenate([jnp.ones((Tp, QK_NOPE), f32), cos, cos, jnp.zeros((Tp, QK_ROPE), f32)], axis=1)
    stab = jnp.concatenate([jnp.zeros((Tp, QK_NOPE), f32), sin, sin, jnp.zeros((Tp, QK_ROPE), f32)], axis=1)

    TM = Tp // SEQ_CHUNKS
    npb = Tp // TM
    tok = lambda width: pl.BlockSpec((TM, width), lambda i: (i, 0))
    return pl.pallas_call(
        _proj_kernel,
        grid=(N // TM,),
        in_specs=[tok(D), _const_spec((1, D)), _const_spec(w1.shape), _const_spec(w2.shape),
                  _const_spec((1, Q_LORA)), _const_spec((1, KV_LORA)),
                  _const_spec(wuq_ext.shape), _const_spec(wuk_ext.shape), _const_spec(wuv.shape),
                  pl.BlockSpec((TM, HEAD_PAD), lambda i: (i % npb, 0)),
                  pl.BlockSpec((TM, HEAD_PAD), lambda i: (i % npb, 0))],
        out_specs=[tok(D), tok(D), tok(D), tok(D), tok(H * HEAD_PAD), tok(H * HEAD_PAD), tok(H * V_DIM)],
        out_shape=[jax.ShapeDtypeStruct((N, D), bf16)] * 4
                  + [jax.ShapeDtypeStruct((N, H * HEAD_PAD), bf16)] * 2
                  + [jax.ShapeDtypeStruct((N, H * V_DIM), bf16)],
        compiler_params=pltpu.CompilerParams(dimension_semantics=("parallel",), vmem_limit_bytes=VMEM_LIMIT),
        name="proj",
    )(h0, _row(norm1_g), w1, w2, _row(q_norm_g), _row(kv_norm_g), wuq_ext, wuk_ext, wuv, ctab, stab)


def _rglru_kernel(xr_ref, gr_ref, grnn_ref, cw_ref, cb_ref, wa_ref, ba_ref, wx_ref, bx_ref,
                  lam_ref, wo_ref, out_ref, xe_ref, hs_ref, hseq_ref, *, front):
    Tc, D = xr_ref.shape
    c = pl.program_id(1)

    @pl.when(c == 0)
    def _():
        xe_ref[0:SUBLANES, :] = jnp.zeros((SUBLANES, D), jnp.float32)
        hs_ref[...] = jnp.zeros_like(hs_ref)

    x = xr_ref[...].astype(jnp.float32)
    xe_ref[SUBLANES:, :] = x
    y = cb_ref[...] + x * cw_ref[CONV_W - 1:CONV_W, :]
    for kk in range(CONV_W - 1):
        off = SUBLANES - (CONV_W - 1) + kk
        y = y + xe_ref[off:off + Tc, :] * cw_ref[kk:kk + 1, :]
    xe_ref[0:SUBLANES, :] = xe_ref[Tc:Tc + SUBLANES, :]

    yb = _bf(y)
    bs = D // RG_BLOCKS
    r = jnp.concatenate([_dot(yb[:, n * bs:(n + 1) * bs], wa_ref[n]) for n in range(RG_BLOCKS)], axis=1)
    i = jnp.concatenate([_dot(yb[:, n * bs:(n + 1) * bs], wx_ref[n]) for n in range(RG_BLOCKS)], axis=1)
    r = jax.nn.sigmoid(r + ba_ref[...])
    i = jax.nn.sigmoid(i + bx_ref[...])
    lam = lam_ref[...]
    log_sig = jnp.minimum(lam, 0.0) - jnp.log(1.0 + jnp.exp(-jnp.abs(lam)))
    log_a = RG_C * r * log_sig
    a = jnp.exp(log_a)
    u = jnp.sqrt(1.0 - a * a) * (i * y)

    row = lax.broadcasted_iota(jnp.int32, (Tc, D), 0)
    u = jnp.where(row + c * Tc >= front, u, 0.0)
    groups = Tc // SUBLANES
    a = a.reshape(groups, SUBLANES, D)
    u = u.reshape(groups, SUBLANES, D)
    sub = lax.broadcasted_iota(jnp.int32, (groups, SUBLANES, D), 1)
    d = 1
    while d < SUBLANES:
        keep = sub >= d
        a_prev = jnp.where(keep, pltpu.roll(a, d, 1), 1.0)
        u_prev = jnp.where(keep, pltpu.roll(u, d, 1), 0.0)
        u = a * u_prev + u
        a = a * a_prev
        d *= 2
    carry = hs_ref[...]
    for g in range(groups):
        hg = u[g] + a[g] * carry
        hseq_ref[g * SUBLANES:(g + 1) * SUBLANES, :] = hg
        carry = jnp.broadcast_to(hg[SUBLANES - 1:SUBLANES, :], (SUBLANES, D))
    hs_ref[...] = carry

    gate = jax.nn.gelu(gr_ref[...].astype(jnp.float32))
    y_rnn = _dot(_bf(hseq_ref[...] * gate), wo_ref[...])
    out_ref[...] = _bf(jax.nn.sigmoid(grnn_ref[...].astype(jnp.float32)) * y_rnn)


def _stage_rglru(xr, gr, grnn, B, Tp, front, conv_w, conv_b, rg_wa, rg_ba, rg_wx, rg_bx, rg_lambda, w_rnn_out):
    N, D = xr.shape
    Tc = Tp // SEQ_CHUNKS
    ncb = Tp // Tc
    seq = lambda: pl.BlockSpec((Tc, D), lambda b, c: (b * ncb + c, 0))
    return pl.pallas_call(
        functools.partial(_rglru_kernel, front=front),
        grid=(B, ncb),
        in_specs=[seq(), seq(), seq(), _const_spec((CONV_W, D)), _const_spec((1, D)),
                  _const_spec(rg_wa.shape), _const_spec((1, D)),
                  _const_spec(rg_wx.shape), _const_spec((1, D)), _const_spec((1, D)),
                  _const_spec((D, D))],
        out_specs=seq(),
        out_shape=jax.ShapeDtypeStruct((N, D), jnp.bfloat16),
        scratch_shapes=[pltpu.VMEM((Tc + SUBLANES, D), jnp.float32), pltpu.VMEM((SUBLANES, D), jnp.float32),
                        pltpu.VMEM((Tc, D), jnp.float32)],
        compiler_params=pltpu.CompilerParams(dimension_semantics=("parallel", "arbitrary"),
                                             vmem_limit_bytes=VMEM_LIMIT),
        name="rglru",
    )(xr, gr, grnn, conv_w, _row(conv_b), _bf(rg_wa), _row(rg_ba), _bf(rg_wx), _row(rg_bx),
      _row(rg_lambda), _bf(w_rnn_out))


def _attn_kernel(q_ref, k_ref, v_ref, o_ref, s_ref, p_ref, vt_ref, ot_ref, *, q_blocks, front):
    f32 = jnp.float32
    vt_ref[...] = _bf(jnp.transpose(v_ref[0].astype(f32)))
    for (qs, qn) in q_blocks:
        kl = qs + qn
        for j in range(2):
            sj, pj = s_ref.at[j], p_ref.at[j]
            qh = q_ref[0, qs:qs + qn, j * HEAD_PAD:(j + 1) * HEAD_PAD]
            kh = k_ref[0, 0:kl, j * HEAD_PAD:(j + 1) * HEAD_PAD]
            sj[0:kl, 0:qn] = _dot_nt(kh, qh)
            key = lax.broadcasted_iota(jnp.int32, (qn, qn), 0)
            qry = lax.broadcasted_iota(jnp.int32, (qn, qn), 1)
            sj[qs:kl, 0:qn] = jnp.where(key <= qry, sj[qs:kl, 0:qn], -1e30)
            pad_rows = -(-front // SUBLANES) * SUBLANES
            is_key = lax.broadcasted_iota(jnp.int32, (pad_rows, qn), 0) >= front
            sj[0:pad_rows, 0:qn] = jnp.where(is_key, sj[0:pad_rows, 0:qn], -1e30)
            m8 = jnp.full((SUBLANES, qn), -jnp.inf, f32)
            for r in range(0, kl, ATTN_ROWS):
                blk = sj[r:r + ATTN_ROWS, 0:qn].reshape(ATTN_ROWS // SUBLANES, SUBLANES, qn)
                m8 = jnp.maximum(m8, jnp.max(blk, axis=0))
            m = jnp.max(m8, axis=0, keepdims=True)
            l8 = jnp.zeros((SUBLANES, qn), f32)
            for r in range(0, kl, ATTN_ROWS):
                p = jnp.exp2(sj[r:r + ATTN_ROWS, 0:qn] - m)
                l8 = l8 + jnp.sum(p.reshape(ATTN_ROWS // SUBLANES, SUBLANES, qn), axis=0)
                pj[r:r + ATTN_ROWS, 0:qn] = _bf(p)
            l = jnp.sum(l8, axis=0, keepdims=True)
            ot = _dot(vt_ref[j * V_DIM:(j + 1) * V_DIM, 0:kl], pj[0:kl, 0:qn])
            ot_ref[j * V_DIM:(j + 1) * V_DIM, qs:qs + qn] = ot / l
    o_ref[0] = _bf(jnp.transpose(ot_ref[...]))


def _q_blocks(tp, qb):
    blocks = []
    s = 0
    while s < tp:
        n = min(qb, tp - s)
        blocks.append((s, n))
        s += n
    return tuple(blocks)


def _stage_attn(q, k, v, B, Tp, front):
    H = MLA_HEADS
    q3 = q.reshape(B, Tp, H * HEAD_PAD)
    k3 = k.reshape(B, Tp, H * HEAD_PAD)
    v3 = v.reshape(B, Tp, H * V_DIM)
    o = pl.pallas_call(
        functools.partial(_attn_kernel, q_blocks=_q_blocks(Tp, ATTN_QBLOCK), front=front),
        grid=(B, H // 2),
        in_specs=[pl.BlockSpec((1, Tp, 2 * HEAD_PAD), lambda b, hp: (b, 0, hp)),
                  pl.BlockSpec((1, Tp, 2 * HEAD_PAD), lambda b, hp: (b, 0, hp)),
                  pl.BlockSpec((1, Tp, 2 * V_DIM), lambda b, hp: (b, 0, hp))],
        out_specs=pl.BlockSpec((1, Tp, 2 * V_DIM), lambda b, hp: (b, 0, hp)),
        out_shape=jax.ShapeDtypeStruct((B, Tp, H * V_DIM), jnp.bfloat16),
        scratch_shapes=[pltpu.VMEM((2, Tp, ATTN_QBLOCK), jnp.float32),
                        pltpu.VMEM((2, Tp, ATTN_QBLOCK), jnp.bfloat16),
                        pltpu.VMEM((2 * V_DIM, Tp), jnp.bfloat16), pltpu.VMEM((2 * V_DIM, Tp), jnp.float32)],
        compiler_params=pltpu.CompilerParams(dimension_semantics=("parallel", "parallel"),
                                             vmem_limit_bytes=VMEM_LIMIT),
        name="attn",
    )(q3, k3, v3)
    return o.reshape(B * Tp, H * V_DIM)


def _mark(k):
    return _TAKEN * (1.0 + jnp.asarray(k, jnp.float32) * (1.0 / 32.0))


def _take_topk(entry, nkeys, emit):
    def step(k, carry):
        vals = [entry(key)[...] for key in range(nkeys)]
        idxs = [float(key) for key in range(nkeys)]
        while len(vals) > 1:
            nv, ni = [], []
            for j in range(0, len(vals) - 1, 2):
                right = vals[j + 1] > vals[j]
                nv.append(jnp.where(right, vals[j + 1], vals[j]))
                ni.append(jnp.where(right, idxs[j + 1], idxs[j]))
            if len(vals) % 2:
                nv.append(vals[-1])
                ni.append(idxs[-1])
            vals, idxs = nv, ni
        m, first = vals[0], idxs[0]
        emit(k, m)
        mark = _mark(k)
        for key in range(nkeys):
            ref = entry(key)
            ref[...] = jnp.where(first == float(key), mark, ref[...])
        return carry

    lax.fori_loop(0, PEER_TOPK, step, 0)


def _count_taken(entry, nkeys):
    n = None
    for key in range(nkeys):
        one = jnp.where(entry(key)[...] <= _TAKEN, 1.0, 0.0)
        n = one if n is None else n + one
    return n


def _merge_kernel(ha_ref, hb_ref, yra_ref, yrb_ref, oa_ref, ob_ref, ga_ref, gb_ref,
                  wao_ref, wout_ref, g2_ref, wq_ref, k1_ref, k2_ref,
                  h1_ref, n2t_ref, c_ref, q1_ref, r2_ref, e2_ref,
                  s_ref, e_ref, r_ref, t_ref, cd_ref, t2_ref, qp_ref):
    TM = h1_ref.shape[0]
    LG = TM // LANES
    NK = PEER_NKEYS
    H = PEER_HEADS
    f32 = jnp.float32

    def rows(a_ref, b_ref):
        return jnp.concatenate([a_ref[...], b_ref[...]], axis=0)

    y_att = _dot(rows(oa_ref, ob_ref), wao_ref[...])
    mixed = rows(yra_ref, yrb_ref).astype(f32) + jax.nn.sigmoid(rows(ga_ref, gb_ref).astype(f32)) * y_att
    h1 = rows(ha_ref, hb_ref) + _dot(_bf(mixed), wout_ref[...])
    h1_ref[...] = h1
    n2 = _bf(_rms(h1, g2_ref[...]))
    n2t_ref[...] = _bf(jnp.transpose(n2.astype(f32)))
    qp_ref[...] = _bf(_dot(n2, wq_ref[...]))

    def slab(ref, side, key):
        return ref.at[:, side * NKH + key * H:side * NKH + (key + 1) * H, :]

    def select(take):
        for h in range(H):
            for side, kref in enumerate((k1_ref, k2_ref)):
                qs = qp_ref[:, (2 * h + side) * NK:(2 * h + side + 1) * NK]
                st = _dot_nt(kref[...], qs)
                for lg in range(LG):
                    s_ref[lg, pl.ds(side * NKH + h, NK, stride=H), :] = st[:, lg * LANES:(lg + 1) * LANES]
        for side in range(2):
            rows = slice(side * NKH, (side + 1) * NKH)
            s0 = s_ref[:, rows, :].reshape(LG, NK, H, LANES)
            e_ref[:, rows, :] = jnp.exp(s0 - jnp.max(s0, axis=1, keepdims=True)).reshape(LG, NKH, LANES)
        for side in range(2):
            def emit(k, m, side=side):
                t_ref[k, side] = m
            take(lambda key, side=side: slab(s_ref, side, key), NK, emit)
        for p, (k1, k2) in enumerate(_CAND):
            cd_ref[p] = t_ref[k1, 0] + t_ref[k2, 1]

        def emit2(k, m):
            t2_ref[k] = m
        take(lambda p: cd_ref.at[p], _NCAND, emit2)

    select(_take_topk)

    top = t2_ref[0]
    z = jnp.exp(t2_ref[0] - top)
    for k in range(1, PEER_TOPK):
        z = z + jnp.exp(t2_ref[k] - top)
    inv_z = 1.0 / z
    counts = [_count_taken(lambda q, k1=k1: cd_ref.at[_ROW_START[k1] + q], _ROW_START[k1 + 1] - _ROW_START[k1])
              for k1 in range(PEER_TOPK)]

    def step_of(s):
        return jnp.where(s <= _TAKEN, (s * (1.0 / _TAKEN) - 1.0) * 32.0, float(PEER_TOPK))

    for key in range(NK):
        rank1 = step_of(slab(s_ref, 0, key)[...])
        cslab = jnp.zeros((LG, H, LANES), f32)
        for k1 in range(PEER_TOPK):
            cslab = jnp.where(rank1 == float(k1), counts[k1], cslab)
        qslab = slab(e_ref, 0, key)[...] * inv_z
        rank2 = step_of(slab(s_ref, 1, key)[...])
        r_ref[:, key * H:(key + 1) * H, :] = rank2
        for lg in range(LG):
            cols = slice(lg * LANES, (lg + 1) * LANES)
            c_ref[key * H:(key + 1) * H, cols] = cslab[lg]
            q1_ref[key * H:(key + 1) * H, cols] = qslab[lg]
    for lg in range(LG):
        cols = slice(lg * LANES, (lg + 1) * LANES)
        for h in range(H):
            r2_ref[h * NK:(h + 1) * NK, cols] = _bf(r_ref[lg, pl.ds(h, NK, stride=H), :])
            e2_ref[h * NK:(h + 1) * NK, cols] = _bf(e_ref[lg, pl.ds(NKH + h, NK, stride=H), :])


def _stage_merge(h0, yrg, o, gatt, B, Tp, S, w_attn_out, w_out, norm2_g, peer_wq, peer_keys1, peer_keys2):
    D = h0.shape[1]
    N = B * S
    f32 = jnp.float32
    TM = MERGE_TOKENS
    LG = TM // LANES
    HB = TM // 2
    assert S % TM == 0 and (Tp - S) % HB == 0
    tiles = S // TM
    first = (Tp - S) // HB

    def half(which):
        return pl.BlockSpec((None, HB, D), lambda i: (i // tiles, first + 2 * (i % tiles) + which, 0))

    tok_in = [half(0), half(1)] * 4
    as3 = lambda a: a.reshape(B, Tp, D)
    tok = lambda width: pl.BlockSpec((TM, width), lambda i: (i, 0))
    colb = lambda rows: pl.BlockSpec((rows, TM), lambda i: (0, i))
    return pl.pallas_call(
        _merge_kernel,
        grid=(N // TM,),
        in_specs=tok_in + [_const_spec((D, D)), _const_spec((D, D)),
                           _const_spec((1, D)), _const_spec(peer_wq.shape),
                           _const_spec(peer_keys1.shape), _const_spec(peer_keys2.shape)],
        out_specs=[tok(D), colb(D), colb(NKH), colb(NKH), colb(NKH), colb(NKH)],
        out_shape=[jax.ShapeDtypeStruct((N, D), f32), jax.ShapeDtypeStruct((D, N), jnp.bfloat16)]
                  + [jax.ShapeDtypeStruct((NKH, N), f32)] * 2
                  + [jax.ShapeDtypeStruct((NKH, N), jnp.bfloat16)] * 2,
        scratch_shapes=[pltpu.VMEM((LG, 2 * NKH, LANES), f32), pltpu.VMEM((LG, 2 * NKH, LANES), f32),
                        pltpu.VMEM((LG, NKH, LANES), f32),
                        pltpu.VMEM((PEER_TOPK, 2, LG, PEER_HEADS, LANES), f32),
                        pltpu.VMEM((_NCAND, LG, PEER_HEADS, LANES), f32),
                        pltpu.VMEM((PEER_TOPK, LG, PEER_HEADS, LANES), f32),
                        pltpu.VMEM((TM, peer_wq.shape[1]), jnp.bfloat16)],
        compiler_params=pltpu.CompilerParams(dimension_semantics=("parallel",), vmem_limit_bytes=VMEM_LIMIT),
        name="merge",
    )(as3(h0), as3(h0), as3(yrg), as3(yrg), as3(o), as3(o), as3(gatt), as3(gatt),
      _bf(w_attn_out), _bf(w_out), _row(norm2_g), _bf(peer_wq), _bf(peer_keys1), _bf(peer_keys2))


def _gelu_tanh(x):
    c = 0.7978845608028654
    hx = 0.5 * x
    return hx + hx * jnp.tanh(x * (c + (c * 0.044715) * (x * x)))


def _peer_kernel(n2t_ref, c_ref, q1_ref, r2_ref, e2_ref, u_ref, vt_ref, h1_ref, gf_ref,
                 out_ref, acc_ref, act_ref, coef_ref):
    j = pl.program_id(1)
    ET = u_ref.shape[0]
    TT = n2t_ref.shape[1]
    NK = PEER_NKEYS
    H = PEER_HEADS
    groups = ET // NK

    @pl.when(j == 0)
    def _():
        acc_ref[...] = jnp.zeros_like(acc_ref)

    act_ref[...] = _bf(_dot(u_ref[...], n2t_ref[...]))
    BW = PEER_BLOCK_LANES
    packed = (NK // BF16_ROWS, BF16_ROWS, BW)
    zero = jnp.zeros(packed, jnp.bfloat16)
    for g in range(groups):
        crow = c_ref[g * H:(g + 1) * H, :]
        qrow = q1_ref[g * H:(g + 1) * H, :]
        for lb in range(TT // BW):
            cols = slice(lb * BW, (lb + 1) * BW)
            w = None
            for h in range(H):
                r2 = r2_ref[h * NK:(h + 1) * NK, cols].reshape(packed)
                e2 = e2_ref[h * NK:(h + 1) * NK, cols].reshape(packed)
                cb = _bf(jnp.broadcast_to(crow[h:h + 1, cols], (BF16_ROWS, BW)))
                qb = _bf(jnp.broadcast_to(qrow[h:h + 1, cols], (BF16_ROWS, BW)))
                term = jnp.where(r2 < jnp.broadcast_to(cb[None], packed), e2 * qb[None], zero)
                w = term if w is None else w + term
            rows = slice(g * NK, (g + 1) * NK)
            coef_ref[rows, cols] = w.reshape(NK, BW) * _gelu_tanh(act_ref[rows, cols])
    acc_ref[...] += _dot(vt_ref[...], coef_ref[...])

    @pl.when(j == pl.num_programs(1) - 1)
    def _():
        h2 = h1_ref[...] + jnp.transpose(acc_ref[...])
        out_ref[...] = _rms(h2, gf_ref[...])


def _stage_peer(n2t, cc, q1, r2, e2, h1, peer_u, peer_v, final_g):
    N, D = h1.shape
    TT = PEER_TOKENS
    ET = PEER_EXPERT_TILE
    assert N % TT == 0 and peer_u.shape[0] % ET == 0
    NE = peer_u.shape[0]
    u_bf = _bf(peer_u)
    vt_bf = _bf(jnp.transpose(peer_v))
    colt = lambda rows: pl.BlockSpec((rows, TT), lambda i, j: (0, i))
    grp = pl.BlockSpec((ET // PEER_NKEYS * PEER_HEADS, TT), lambda i, j: (j, i))
    return pl.pallas_call(
        _peer_kernel,
        grid=(N // TT, NE // ET),
        in_specs=[colt(D), grp, grp, colt(NKH), colt(NKH),
                  pl.BlockSpec((ET, D), lambda i, j: (j, 0)),
                  pl.BlockSpec((D, ET), lambda i, j: (0, j)),
                  pl.BlockSpec((TT, D), lambda i, j: (i, 0), pipeline_mode=pl.Buffered(1)),
                  pl.BlockSpec((1, D), lambda i, j: (0, 0))],
        out_specs=pl.BlockSpec((TT, D), lambda i, j: (i, 0)),
        out_shape=jax.ShapeDtypeStruct((N, D), jnp.float32),
        scratch_shapes=[pltpu.VMEM((D, TT), jnp.float32), pltpu.VMEM((ET, TT), jnp.bfloat16),
                        pltpu.VMEM((ET, TT), jnp.bfloat16)],
        compiler_params=pltpu.CompilerParams(dimension_semantics=("parallel", "arbitrary"),
                                             vmem_limit_bytes=VMEM_LIMIT),
        name="peer",
    )(n2t, cc, q1, r2, e2, u_bf, vt_bf, h1, _row(final_g))


def kernel(x, meta_tokens, norm1_g, w_in, conv_w, conv_b, rg_wa, rg_ba, rg_wx, rg_bx, rg_lambda,
           w_rnn_out, q_norm_g, w_uq, kv_norm_g, w_ukv, w_attn_out, w_out, norm2_g,
           peer_wq, peer_keys1, peer_keys2, peer_u, peer_v, final_g):
    B, S, D = x.shape
    assert w_in.shape[0] == 1, "single layer"
    T = N_META + S
    Tp = -(-T // BLOCK) * BLOCK
    N = B * Tp

    front = Tp - T
    meta = jnp.broadcast_to(meta_tokens[None].astype(x.dtype), (B, N_META, D))
    h0 = jnp.concatenate([jnp.zeros((B, front, D), x.dtype), meta, x], axis=1).reshape(N, D)

    xr, gr, grnn, gatt, q, k, v = _stage_proj(h0, Tp, front, norm1_g[0], w_in[0], q_norm_g[0], w_uq[0],
                                              kv_norm_g[0], w_ukv[0])
    yrg = _stage_rglru(xr, gr, grnn, B, Tp, front, conv_w[0], conv_b[0], rg_wa[0], rg_ba[0], rg_wx[0],
                       rg_bx[0], rg_lambda[0], w_rnn_out[0])
    o = _stage_attn(q, k, v, B, Tp, front)
    h1, n2t, cc, q1, r2, e2 = _stage_merge(h0, yrg, o, gatt, B, Tp, S, w_attn_out[0], w_out[0], norm2_g[0],
                                           peer_wq[0], peer_keys1[0], peer_keys2[0])
    out = _stage_peer(n2t, cc, q1, r2, e2, h1, peer_u[0], peer_v[0], final_g)
    return out.reshape(B, S, D)
```
